```python
import math
import jax, jax.numpy as jnp
from jax import lax
import numpy as np

D_MODEL = 1024
BATCH = 8
SEQ = 2048
DEPTH = 1

CHUNK = 64
Q_BLOCK = 128
HEAD_DIM = 64
N_DIFF_HEADS = 4
DIFF_V_DIM = 2 * HEAD_DIM
N_SB_HEADS = 8
DIFF_WIDTH = N_DIFF_HEADS * DIFF_V_DIM
SB_WIDTH = N_SB_HEADS * HEAD_DIM
MIX_WIDTH = DIFF_WIDTH + SB_WIDTH
DIFF_QK_WIDTH = N_DIFF_HEADS * 2 * HEAD_DIM
IN_SPLITS = (DIFF_QK_WIDTH, 2 * DIFF_QK_WIDTH, 2 * DIFF_QK_WIDTH + DIFF_WIDTH,
             2 * DIFF_QK_WIDTH + DIFF_WIDTH + SB_WIDTH,
             2 * DIFF_QK_WIDTH + DIFF_WIDTH + 2 * SB_WIDTH)
IN_WIDTH = 2 * DIFF_QK_WIDTH + DIFF_WIDTH + 3 * SB_WIDTH
D_FF = 4 * D_MODEL
EPS = 1e-6
NEG_INF = -1e30

kernel_name = "hymba_diffattn_stickbreak_sqrelu_block"


def rms_norm(x, g):
    xf = x.astype(jnp.float32)
    y = xf * lax.rsqrt(jnp.mean(xf * xf, axis=-1, keepdims=True) + EPS)
    return (y * g.astype(jnp.float32)).astype(x.dtype)


def alibi_slopes(n_heads):
    return jnp.exp2(-8.0 * jnp.arange(1, n_heads + 1, dtype=jnp.float32) / n_heads)


def diff_attention(q, k, v, lam, lambda_init, subln_g):
    B, S, H = q.shape[0], q.shape[1], q.shape[2]
    scale = 1.0 / math.sqrt(HEAD_DIM)
    slopes = alibi_slopes(H)
    outs = []
    for i in range(S // Q_BLOCK):
        t0, t1 = i * Q_BLOCK, (i + 1) * Q_BLOCK
        qb = q[:, t0:t1].astype(jnp.float32)
        kb = k[:, :t1].astype(jnp.float32)
        scores = jnp.einsum('bqhmd,bkhmd->bhmqk', qb, kb) * scale
        tpos = jnp.arange(t0, t1)
        spos = jnp.arange(t1)
        mask = (spos // CHUNK)[None, :] <= (tpos // CHUNK)[:, None]
        dist = jnp.abs(tpos[:, None] - spos[None, :]).astype(jnp.float32)
        bias = -slopes[:, None, None] * dist
        scores = jnp.where(mask[None, None, None], scores + bias[None, :, None], NEG_INF)
        p = jax.nn.softmax(scores, axis=-1)
        attn = p[:, :, 0] - lam * p[:, :, 1]
        outs.append(jnp.einsum('bhqk,bkhe->bqhe', attn, v[:, :t1].astype(jnp.float32)))
    o = jnp.concatenate(outs, axis=1)
    o = rms_norm(o, subln_g) * (1.0 - lambda_init)
    return o.reshape(B, S, H * DIFF_V_DIM).astype(q.dtype)


def stick_breaking(q, k, v, norm_g):
    B, S, H = q.shape[0], q.shape[1], q.shape[2]
    scale = 1.0 / math.sqrt(HEAD_DIM)
    outs = []
    for i in range(S // Q_BLOCK):
        t0, t1 = i * Q_BLOCK, (i + 1) * Q_BLOCK
        qb = q[:, t0:t1].astype(jnp.float32)
        kb = k[:, :t1].astype(jnp.float32)
        z = jnp.einsum('bqhd,bkhd->bhqk', qb, kb) * scale
        tpos = jnp.arange(t0, t1)
        spos = jnp.arange(t1)
        mask = (spos[None, :] < tpos[:, None])[None, None]
        neg_log_1m_beta = jnp.where(mask, jax.nn.softplus(z), 0.0)
        suffix = lax.cumsum(neg_log_1m_beta, axis=3, reverse=True)
        log_a = jax.nn.log_sigmoid(z) - (suffix - neg_log_1m_beta)
        a = jnp.where(mask, jnp.exp(log_a), 0.0)
        outs.append(jnp.einsum('bhqk,bkhd->bqhd', a, v[:, :t1].astype(jnp.float32)))
    o = jnp.concatenate(outs, axis=1)
    o = rms_norm(o, norm_g)
    return o.reshape(B, S, H * HEAD_DIM).astype(q.dtype)


def setup_inputs(seed: int = 0) -> dict:
    key = jax.random.key(seed)
    ks = jax.random.split(key, 16)
    f32 = jnp.float32
    def nrm(k, shape, scale):
        return jax.random.normal(k, shape, f32) * scale
    return {
        "x": nrm(ks[0], (BATCH, SEQ, D_MODEL), 1.0),
        "norm1_g": 1.0 + nrm(ks[1], (DEPTH, D_MODEL), 0.02),
        "w_in": nrm(ks[2], (DEPTH, D_MODEL, IN_WIDTH), D_MODEL ** -0.5),
        "lambda_q1": nrm(ks[3], (DEPTH, HEAD_DIM), 0.1),
        "lambda_k1": nrm(ks[4], (DEPTH, HEAD_DIM), 0.1),
        "lambda_q2": nrm(ks[5], (DEPTH, HEAD_DIM), 0.1),
        "lambda_k2": nrm(ks[6], (DEPTH, HEAD_DIM), 0.1),
        "diff_subln_g": 1.0 + nrm(ks[7], (DEPTH, DIFF_V_DIM), 0.02),
        "sb_norm_g": 1.0 + nrm(ks[8], (DEPTH, HEAD_DIM), 0.02),
        "w_out": nrm(ks[9], (DEPTH, MIX_WIDTH, D_MODEL), MIX_WIDTH ** -0.5),
        "norm2_g": 1.0 + nrm(ks[10], (DEPTH, D_MODEL), 0.02),
        "w_up": nrm(ks[11], (DEPTH, D_MODEL, D_FF), D_MODEL ** -0.5),
        "w_down": nrm(ks[12], (DEPTH, D_FF, D_MODEL), D_FF ** -0.5),
        "final_norm_g": 1.0 + nrm(ks[13], (D_MODEL,), 0.02),
    }


def reference(x, norm1_g, w_in, lambda_q1, lambda_k1, lambda_q2, lambda_k2,
              diff_subln_g, sb_norm_g, w_out, norm2_g, w_up, w_down, final_norm_g):
    B, S = x.shape[0], x.shape[1]
    h = x
    for l in range(DEPTH):
        lambda_init = 0.8 - 0.6 * math.exp(-0.3 * l)
        n = rms_norm(h, norm1_g[l])
        proj = jnp.einsum('bsd,de->bse', n, w_in[l])
        dq, dk, dv, sq, sk, sv = jnp.split(proj, IN_SPLITS, axis=-1)
        dq = dq.reshape(B, S, N_DIFF_HEADS, 2, HEAD_DIM)
        dk = dk.reshape(B, S, N_DIFF_HEADS, 2, HEAD_DIM)
        dv = dv.reshape(B, S, N_DIFF_HEADS, DIFF_V_DIM)
        sq = sq.reshape(B, S, N_SB_HEADS, HEAD_DIM)
        sk = sk.reshape(B, S, N_SB_HEADS, HEAD_DIM)
        sv = sv.reshape(B, S, N_SB_HEADS, HEAD_DIM)
        lam = (jnp.exp(jnp.sum(lambda_q1[l].astype(jnp.float32) * lambda_k1[l].astype(jnp.float32)))
               - jnp.exp(jnp.sum(lambda_q2[l].astype(jnp.float32) * lambda_k2[l].astype(jnp.float32)))
               + lambda_init)
        o_diff = diff_attention(dq, dk, dv, lam, lambda_init, diff_subln_g[l])
        o_sb = stick_breaking(sq, sk, sv, sb_norm_g[l])
        mixed = jnp.concatenate([o_diff, o_sb], axis=-1)
        h = h + jnp.einsum('bse,ed->bsd', mixed, w_out[l])
        n2 = rms_norm(h, norm2_g[l])
        u = jnp.einsum('bsd,df->bsf', n2, w_up[l])
        u = jnp.square(jax.nn.relu(u))
        h = h + jnp.einsum('bsf,fd->bsd', u, w_down[l])
    return rms_norm(h, final_norm_g)
```

```python
import functools
import math

import numpy as np
import jax
import jax.numpy as jnp
from jax import lax
from jax.experimental import pallas as pl
from jax.experimental.pallas import tpu as pltpu

D_MODEL = 1024
HEAD_DIM = 64
CHUNK = 64
N_DIFF_HEADS = 4
N_SB_HEADS = 8
DIFF_V_DIM = 2 * HEAD_DIM
DIFF_WIDTH = N_DIFF_HEADS * DIFF_V_DIM
SB_WIDTH = N_SB_HEADS * HEAD_DIM
IN_WIDTH = 3 * DIFF_WIDTH + 3 * SB_WIDTH
D_FF = 4 * D_MODEL
EPS = 1e-6
NEG_INF = -1e30
LAMBDA_INIT = 0.8 - 0.6 * math.exp(-0.3 * 0)
LOG2E = 1.4426950408889634

LANES = 128
TQ = 256
TK = 128
TM = 512
VMEM_LIMIT = 56 * 1024 * 1024

F32 = jnp.float32
BF16 = jnp.bfloat16


def _nt_dot(a, b):
    return lax.dot_general(a, b, (((1,), (1,)), ((), ())), preferred_element_type=F32)


def _half_masks(dtype):
    lane = lax.broadcasted_iota(jnp.int32, (1, LANES), 1)
    lo = jnp.where(lane < HEAD_DIM, 1.0, 0.0).astype(dtype)
    hi = jnp.where(lane >= HEAD_DIM, 1.0, 0.0).astype(dtype)
    return lo, hi


def _norm_proj_kernel(x_ref, g_ref, w_ref, cs_ref, o_ref, *, n_chunk):
    x = x_ref[...]
    ms = jnp.mean(x * x, axis=-1, keepdims=True)
    n = (x * lax.rsqrt(ms + EPS) * g_ref[...]).astype(BF16)
    for c in range(IN_WIDTH // n_chunk):
        sl = slice(c * n_chunk, (c + 1) * n_chunk)
        acc = jnp.dot(n, w_ref[:, sl], preferred_element_type=F32)
        o_ref[:, sl] = (acc * cs_ref[:, sl]).astype(BF16)


def _norm_proj(x2, g, w_bf, colscale):
    rows = x2.shape[0]
    return pl.pallas_call(
        functools.partial(_norm_proj_kernel, n_chunk=512),
        grid=(rows // TM,),
        in_specs=[
            pl.BlockSpec((TM, D_MODEL), lambda i: (i, 0)),
            pl.BlockSpec((1, D_MODEL), lambda i: (0, 0)),
            pl.BlockSpec((D_MODEL, IN_WIDTH), lambda i: (0, 0)),
            pl.BlockSpec((1, IN_WIDTH), lambda i: (0, 0)),
        ],
        out_specs=pl.BlockSpec((TM, IN_WIDTH), lambda i: (i, 0)),
        out_shape=jax.ShapeDtypeStruct((rows, IN_WIDTH), BF16),
        compiler_params=pltpu.CompilerParams(
            dimension_semantics=("arbitrary",), vmem_limit_bytes=VMEM_LIMIT),
        name="norm_proj",
    )(x2, g, w_bf, colscale)


def _diff_attn_kernel(slopes_ref, lq1_ref, lk1_ref, lq2_ref, lk2_ref, g_ref,
                      q_ref, k_ref, v_ref, o_ref):
    h = pl.program_id(1)
    i = pl.program_id(2)
    t0 = i * TQ
    sl2 = slopes_ref[h]

    lam = (jnp.exp(jnp.sum(lq1_ref[...] * lk1_ref[...], axis=1, keepdims=True))
           - jnp.exp(jnp.sum(lq2_ref[...] * lk2_ref[...], axis=1, keepdims=True))
           + LAMBDA_INIT)

    q = q_ref[0]
    lo, hi = _half_masks(BF16)

    r = lax.broadcasted_iota(jnp.int32, (TQ, TK), 0)
    c = lax.broadcasted_iota(jnp.int32, (TQ, TK), 1)
    rel = (r - c).astype(F32)
    bias_full = -sl2 * rel

    def diag_bias(d):
        cc = c + d * TK
        vis = (cc // CHUNK) <= (r // CHUNK)
        dist = jnp.abs(r - cc).astype(F32)
        return jnp.where(vis, -sl2 * dist, NEG_INF)

    def block(s0, bias, shift, carry):
        m1, m2, l1, l2, acc = carry
        kb = k_ref[0, pl.ds(s0, TK), :]
        vb = v_ref[0, pl.ds(s0, TK), :]
        kbd = jnp.concatenate([kb * lo, kb * hi], axis=0)
        s = _nt_dot(q, kbd)
        u1 = s[:, :TK] + bias
        u2 = s[:, TK:] + bias
        m1n = jnp.maximum(m1, jnp.max(u1, axis=1, keepdims=True) + shift)
        m2n = jnp.maximum(m2, jnp.max(u2, axis=1, keepdims=True) + shift)
        p1 = jnp.exp2(u1 - (m1n - shift))
        p2 = jnp.exp2(u2 - (m2n - shift))
        a1 = jnp.exp2(m1 - m1n)
        a2 = jnp.exp2(m2 - m2n)
        l1 = a1 * l1 + jnp.sum(p1, axis=1, keepdims=True)
        l2 = a2 * l2 + jnp.sum(p2, axis=1, keepdims=True)
        zero = jnp.zeros_like(vb)
        vbd = jnp.concatenate(
            [jnp.concatenate([vb, zero], axis=1), jnp.concatenate([zero, vb], axis=1)], axis=0)
        p = jnp.concatenate([p1, p2], axis=1).astype(BF16)
        pv = jnp.dot(p, vbd, preferred_element_type=F32)
        alpha = jnp.concatenate([jnp.broadcast_to(a1, (TQ, LANES)),
                                 jnp.broadcast_to(a2, (TQ, LANES))], axis=1)
        acc = acc * alpha + pv
        return m1n, m2n, l1, l2, acc

    init = (jnp.full((TQ, 1), NEG_INF, F32), jnp.full((TQ, 1), NEG_INF, F32),
            jnp.zeros((TQ, 1), F32), jnp.zeros((TQ, 1), F32),
            jnp.zeros((TQ, 2 * LANES), F32))
    carry = block(pl.multiple_of(t0, TK), diag_bias(0), 0.0, init)
    carry = block(pl.multiple_of(t0 + TK, TK), diag_bias(1), 0.0, carry)

    def body(j, carry):
        s0 = pl.multiple_of(j * TK, TK)
        shift = -sl2 * (t0 - s0).astype(F32)
        return block(s0, bias_full, shift, carry)

    m1, m2, l1, l2, acc = lax.fori_loop(0, 2 * i, body, carry)
    o = acc[:, :LANES] / l1 - lam * (acc[:, LANES:] / l2)
    ms = jnp.mean(o * o, axis=-1, keepdims=True)
    o = o * lax.rsqrt(ms + EPS) * g_ref[...] * (1.0 - LAMBDA_INIT)
    o_ref[0] = o.astype(BF16)


def _diff_attention(qkv, slopes2, lq1, lk1, lq2, lk2, subln_g):
    B, S, _ = qkv.shape
    kcol = DIFF_WIDTH // LANES
    vcol = 2 * DIFF_WIDTH // LANES
    vec = pl.BlockSpec((1, HEAD_DIM), lambda b, h, i: (0, 0))
    return pl.pallas_call(
        _diff_attn_kernel,
        grid=(B, N_DIFF_HEADS, S // TQ),
        in_specs=[
            pl.BlockSpec(memory_space=pltpu.SMEM),
            vec, vec, vec, vec,
            pl.BlockSpec((1, DIFF_V_DIM), lambda b, h, i: (0, 0)),
            pl.BlockSpec((1, TQ, LANES), lambda b, h, i: (b, i, h)),
            pl.BlockSpec((1, S, LANES), lambda b, h, i: (b, 0, kcol + h)),
            pl.BlockSpec((1, S, LANES), lambda b, h, i: (b, 0, vcol + h)),
        ],
        out_specs=pl.BlockSpec((1, TQ, LANES), lambda b, h, i: (b, i, h)),
        out_shape=jax.ShapeDtypeStruct((B, S, DIFF_WIDTH), BF16),
        compiler_params=pltpu.CompilerParams(
            dimension_semantics=("arbitrary", "arbitrary", "arbitrary"),
            vmem_limit_bytes=VMEM_LIMIT),
        name="diff_attn",
    )(slopes2, lq1, lk1, lq2, lk2, subln_g, qkv, qkv, qkv)


def _stick_break_kernel(g_ref, q_ref, k_ref, v_ref, o_ref):
    i = pl.program_id(2)
    t0 = i * TQ
    q = q_ref[0]
    lo, hi = _half_masks(BF16)

    r = lax.broadcasted_iota(jnp.int32, (TQ, 2 * TK), 0)
    c = lax.broadcasted_iota(jnp.int32, (TQ, 2 * TK), 1) % TK
    ju = lax.broadcasted_iota(jnp.int32, (2 * TK, 2 * TK), 0)
    su = lax.broadcasted_iota(jnp.int32, (2 * TK, 2 * TK), 1)
    ubd = jnp.where((ju >= su) & ((ju // TK) == (su // TK)), 1.0, 0.0).astype(BF16)

    def block(s0, mask, carry):
        ca, cb, acc = carry
        kb = k_ref[0, pl.ds(s0, TK), :]
        vb = v_ref[0, pl.ds(s0, TK), :]
        kbd = jnp.concatenate([kb * lo, kb * hi], axis=0)
        z = _nt_dot(q, kbd)
        sp = jnp.maximum(z, 0.0) + jnp.log2(1.0 + jnp.exp2(-jnp.abs(z)))
        if mask is not None:
            sp = jnp.where(mask, sp, 0.0)
        suf = jnp.dot(sp.astype(BF16), ubd, preferred_element_type=F32)
        cpair = jnp.concatenate([jnp.broadcast_to(ca, (TQ, TK)),
                                 jnp.broadcast_to(cb, (TQ, TK))], axis=1)
        a = jnp.exp2(z - suf - cpair)
        if mask is not None:
            a = jnp.where(mask, a, 0.0)
        vbd = jnp.concatenate([vb * lo, vb * hi], axis=0)
        acc = acc + jnp.dot(a.astype(BF16), vbd, preferred_element_type=F32)
        ca = ca + jnp.sum(sp[:, :TK], axis=1, keepdims=True)
        cb = cb + jnp.sum(sp[:, TK:], axis=1, keepdims=True)
        return ca, cb, acc

    init = (jnp.zeros((TQ, 1), F32), jnp.zeros((TQ, 1), F32), jnp.zeros((TQ, LANES), F32))
    carry = block(pl.multiple_of(t0 + TK, TK), (c + TK) < r, init)
    carry = block(pl.multiple_of(t0, TK), c < r, carry)

    def body(it, carry):
        j = 2 * i - 1 - it
        return block(pl.multiple_of(j * TK, TK), None, carry)

    _, _, o = lax.fori_loop(0, 2 * i, body, carry)
    lane = lax.broadcasted_iota(jnp.int32, (TQ, LANES), 1)
    is_a = lane < HEAD_DIM
    o2 = o * o
    msa = jnp.sum(jnp.where(is_a, o2, 0.0), axis=1, keepdims=True) * (1.0 / HEAD_DIM)
    msb = jnp.sum(jnp.where(is_a, 0.0, o2), axis=1, keepdims=True) * (1.0 / HEAD_DIM)
    ms = jnp.where(is_a, msa, msb)
    o_ref[0] = (o * lax.rsqrt(ms + EPS) * g_ref[...]).astype(BF16)


def _stick_breaking(qkv, g_pair):
    B, S, _ = qkv.shape
    npair = N_SB_HEADS // 2
    qcol = 3 * DIFF_WIDTH // LANES
    kcol = qcol + npair
    vcol = kcol + npair
    return pl.pallas_call(
        _stick_break_kernel,
        grid=(B, npair, S // TQ),
        in_specs=[
            pl.BlockSpec((1, LANES), lambda b, p, i: (0, 0)),
            pl.BlockSpec((1, TQ, LANES), lambda b, p, i: (b, i, qcol + p)),
            pl.BlockSpec((1, S, LANES), lambda b, p, i: (b, 0, kcol + p)),
            pl.BlockSpec((1, S, LANES), lambda b, p, i: (b, 0, vcol + p)),
        ],
        out_specs=pl.BlockSpec((1, TQ, LANES), lambda b, p, i: (b, i, p)),
        out_shape=jax.ShapeDtypeStruct((B, S, SB_WIDTH), BF16),
        compiler_params=pltpu.CompilerParams(
            dimension_semantics=("arbitrary", "arbitrary", "arbitrary"),
            vmem_limit_bytes=VMEM_LIMIT),
        name="stick_break",
    )(g_pair, qkv, qkv, qkv)


def _out_ffn_kernel(x_ref, od_ref, os_ref, wo_ref, g2_ref, wu_ref, wd_ref, gf_ref, o_ref):
    h = (x_ref[...]
         + jnp.dot(od_ref[...], wo_ref[:DIFF_WIDTH, :], preferred_element_type=F32)
         + jnp.dot(os_ref[...], wo_ref[DIFF_WIDTH:, :], preferred_element_type=F32))
    ms = jnp.mean(h * h, axis=-1, keepdims=True)
    n2 = (h * lax.rsqrt(ms + EPS) * g2_ref[...]).astype(BF16)
    u = jnp.dot(n2, wu_ref[...], preferred_element_type=F32)
    u = jnp.square(jnp.maximum(u, 0.0)).astype(BF16)
    h = h + jnp.dot(u, wd_ref[...], preferred_element_type=F32)
    ms = jnp.mean(h * h, axis=-1, keepdims=True)
    o_ref[...] = h * lax.rsqrt(ms + EPS) * gf_ref[...]


def _out_ffn(x2, od, osb, wo, g2, wu, wd, gf):
    rows = x2.shape[0]
    const = lambda i: (0, 0)
    return pl.pallas_call(
        _out_ffn_kernel,
        grid=(rows // TM,),
        in_specs=[
            pl.BlockSpec((TM, D_MODEL), lambda i: (i, 0)),
            pl.BlockSpec((TM, DIFF_WIDTH), lambda i: (i, 0)),
            pl.BlockSpec((TM, SB_WIDTH), lambda i: (i, 0)),
            pl.BlockSpec((D_MODEL, D_MODEL), const),
            pl.BlockSpec((1, D_MODEL), const),
            pl.BlockSpec((D_MODEL, D_FF), const),
            pl.BlockSpec((D_FF, D_MODEL), const),
            pl.BlockSpec((1, D_MODEL), const),
        ],
        out_specs=pl.BlockSpec((TM, D_MODEL), lambda i: (i, 0)),
        out_shape=jax.ShapeDtypeStruct((rows, D_MODEL), F32),
        compiler_params=pltpu.CompilerParams(
            dimension_semantics=("arbitrary",), vmem_limit_bytes=VMEM_LIMIT),
        name="out_ffn",
    )(x2, od, osb, wo, g2, wu, wd, gf)


def kernel(x, norm1_g, w_in, lambda_q1, lambda_k1, lambda_q2, lambda_k2, diff_subln_g,
           sb_norm_g, w_out, norm2_g, w_up, w_down, final_norm_g):
    B, S, D = x.shape
    x2 = x.reshape(B * S, D)

    qs = LOG2E / math.sqrt(HEAD_DIM)
    colscale = np.ones((1, IN_WIDTH), np.float32)
    colscale[0, :DIFF_WIDTH] = qs
    colscale[0, 3 * DIFF_WIDTH:3 * DIFF_WIDTH + SB_WIDTH] = qs
    slopes2 = (np.exp2(-8.0 * np.arange(1, N_DIFF_HEADS + 1) / N_DIFF_HEADS) * LOG2E).astype(np.float32)

    qkv = _norm_proj(x2, norm1_g[0][None, :], w_in[0].astype(BF16), jnp.asarray(colscale))
    qkv = qkv.reshape(B, S, IN_WIDTH)

    o_diff = _diff_attention(qkv, jnp.asarray(slopes2), lambda_q1, lambda_k1, lambda_q2,
                             lambda_k2, diff_subln_g)
    g_pair = jnp.concatenate([sb_norm_g[0], sb_norm_g[0]])[None, :]
    o_sb = _stick_breaking(qkv, g_pair)

    out = _out_ffn(x2, o_diff.reshape(B * S, DIFF_WIDTH), o_sb.reshape(B * S, SB_WIDTH),
                   w_out[0].astype(BF16), norm2_g[0][None, :], w_up[0].astype(BF16),
                   w_down[0].astype(BF16), final_norm_g[None, :])
    return out.reshape(B, S, D)
```

```python
import functools
import math

import numpy as np
import jax
import jax.numpy as jnp
from jax import lax
from jax.experimental import pallas as pl
from jax.experimental.pallas import tpu as pltpu

D_MODEL = 1024
HEAD_DIM = 64
CHUNK = 64
N_DIFF_HEADS = 4
N_SB_HEADS = 8
DIFF_V_DIM = 2 * HEAD_DIM
DIFF_WIDTH = N_DIFF_HEADS * DIFF_V_DIM
SB_WIDTH = N_SB_HEADS * HEAD_DIM
IN_WIDTH = 3 * DIFF_WIDTH + 3 * SB_WIDTH
D_FF = 4 * D_MODEL
EPS = 1e-6
NEG_INF = -1e30
LAMBDA_INIT = 0.8 - 0.6 * math.exp(-0.3 * 0)
LOG2E = 1.4426950408889634

LANES = 128
TQ = 256
TK = 128
TM = 512
VMEM_LIMIT = 56 * 1024 * 1024

F32 = jnp.float32
BF16 = jnp.bfloat16


def _nt_dot(a, b):
    return lax.dot_general(a, b, (((1,), (1,)), ((), ())), preferred_element_type=F32)


def _half_masks(dtype):
    lane = lax.broadcasted_iota(jnp.int32, (1, LANES), 1)
    lo = jnp.where(lane < HEAD_DIM, 1.0, 0.0).astype(dtype)
    hi = jnp.where(lane >= HEAD_DIM, 1.0, 0.0).astype(dtype)
    return lo, hi


def _norm_proj_kernel(x_ref, g_ref, w_ref, cs_ref, o_ref, *, n_chunk):
    x = x_ref[...]
    ms = jnp.mean(x * x, axis=-1, keepdims=True)
    n = (x * lax.rsqrt(ms + EPS) * g_ref[...]).astype(BF16)
    for c in range(IN_WIDTH // n_chunk):
        sl = slice(c * n_chunk, (c + 1) * n_chunk)
        acc = jnp.dot(n, w_ref[:, sl], preferred_element_type=F32)
        o_ref[:, sl] = (acc * cs_ref[:, sl]).astype(BF16)


def _norm_proj(x2, g, w_bf, colscale):
    rows = x2.shape[0]
    return pl.pallas_call(
        functools.partial(_norm_proj_kernel, n_chunk=512),
        grid=(rows // TM,),
        in_specs=[
            pl.BlockSpec((TM, D_MODEL), lambda i: (i, 0)),
            pl.BlockSpec((1, D_MODEL), lambda i: (0, 0)),
            pl.BlockSpec((D_MODEL, IN_WIDTH), lambda i: (0, 0)),
            pl.BlockSpec((1, IN_WIDTH), lambda i: (0, 0)),
        ],
        out_specs=pl.BlockSpec((TM, IN_WIDTH), lambda i: (i, 0)),
        out_shape=jax.ShapeDtypeStruct((rows, IN_WIDTH), BF16),
        compiler_params=pltpu.CompilerParams(
            dimension_semantics=("arbitrary",), vmem_limit_bytes=VMEM_LIMIT),
        name="norm_proj",
    )(x2, g, w_bf, colscale)


def _diff_attn_kernel(slopes_ref, lq1_ref, lk1_ref, lq2_ref, lk2_ref, g_ref,
                      q_ref, k_ref, v_ref, o_ref):
    i = pl.program_id(1)
    t0 = i * TQ

    lam = (jnp.exp(jnp.sum(lq1_ref[...] * lk1_ref[...], axis=1, keepdims=True))
           - jnp.exp(jnp.sum(lq2_ref[...] * lk2_ref[...], axis=1, keepdims=True))
           + LAMBDA_INIT)

    lo, hi = _half_masks(BF16)
    r = lax.broadcasted_iota(jnp.int32, (TQ, TK), 0)
    c = lax.broadcasted_iota(jnp.int32, (TQ, TK), 1)
    rel = (r - c).astype(F32)

    def full_bias(h):
        return -slopes_ref[h] * rel

    def diag_bias(h, d):
        cc = c + d * TK
        vis = (cc // CHUNK) <= (r // CHUNK)
        dist = jnp.abs(r - cc).astype(F32)
        return jnp.where(vis, -slopes_ref[h] * dist, NEG_INF)

    def head_block(h, s0, bias, shift, carry):
        m1, m2, l1, l2, acc = carry
        cols = slice(h * LANES, (h + 1) * LANES)
        q = q_ref[0, :, cols]
        kb = k_ref[0, pl.ds(s0, TK), cols]
        vb = v_ref[0, pl.ds(s0, TK), cols]
        kbd = jnp.concatenate([kb * lo, kb * hi], axis=0)
        s = _nt_dot(q, kbd)
        u1 = s[:, :TK] + bias
        u2 = s[:, TK:] + bias
        m1n = jnp.maximum(m1, jnp.max(u1, axis=1, keepdims=True) + shift)
        m2n = jnp.maximum(m2, jnp.max(u2, axis=1, keepdims=True) + shift)
        p1 = jnp.exp2(u1 - (m1n - shift))
        p2 = jnp.exp2(u2 - (m2n - shift))
        a1 = jnp.exp2(m1 - m1n)
        a2 = jnp.exp2(m2 - m2n)
        l1 = a1 * l1 + jnp.sum(p1, axis=1, keepdims=True)
        l2 = a2 * l2 + jnp.sum(p2, axis=1, keepdims=True)
        zero = jnp.zeros_like(vb)
        vbd = jnp.concatenate(
            [jnp.concatenate([vb, zero], axis=1), jnp.concatenate([zero, vb], axis=1)], axis=0)
        p = jnp.concatenate([p1, p2], axis=1).astype(BF16)
        pv = jnp.dot(p, vbd, preferred_element_type=F32)
        alpha = jnp.concatenate([jnp.broadcast_to(a1, (TQ, LANES)),
                                 jnp.broadcast_to(a2, (TQ, LANES))], axis=1)
        acc = acc * alpha + pv
        return m1n, m2n, l1, l2, acc

    heads = range(N_DIFF_HEADS)
    init = tuple((jnp.full((TQ, 1), NEG_INF, F32), jnp.full((TQ, 1), NEG_INF, F32),
                  jnp.zeros((TQ, 1), F32), jnp.zeros((TQ, 1), F32),
                  jnp.zeros((TQ, 2 * LANES), F32)) for _ in heads)
    carry = init
    for d in range(TQ // TK):
        s0 = pl.multiple_of(t0 + d * TK, TK)
        carry = tuple(head_block(h, s0, diag_bias(h, d), 0.0, carry[h]) for h in heads)

    biases = tuple(full_bias(h) for h in heads)

    def body(j, carry):
        s0 = pl.multiple_of(j * TK, TK)
        dt = (t0 - s0).astype(F32)
        return tuple(head_block(h, s0, biases[h], -slopes_ref[h] * dt, carry[h]) for h in heads)

    carry = lax.fori_loop(0, (TQ // TK) * i, body, carry)
    for h in heads:
        m1, m2, l1, l2, acc = carry[h]
        o = acc[:, :LANES] / l1 - lam * (acc[:, LANES:] / l2)
        ms = jnp.mean(o * o, axis=-1, keepdims=True)
        o = o * lax.rsqrt(ms + EPS) * g_ref[...] * (1.0 - LAMBDA_INIT)
        o_ref[0, :, h * LANES:(h + 1) * LANES] = o.astype(BF16)


def _diff_attention(qkv, slopes2, lq1, lk1, lq2, lk2, subln_g):
    B, S, _ = qkv.shape
    vec = pl.BlockSpec((1, HEAD_DIM), lambda b, i: (0, 0))
    return pl.pallas_call(
        _diff_attn_kernel,
        grid=(B, S // TQ),
        in_specs=[
            pl.BlockSpec(memory_space=pltpu.SMEM),
            vec, vec, vec, vec,
            pl.BlockSpec((1, DIFF_V_DIM), lambda b, i: (0, 0)),
            pl.BlockSpec((1, TQ, DIFF_WIDTH), lambda b, i: (b, i, 0)),
            pl.BlockSpec((1, S, DIFF_WIDTH), lambda b, i: (b, 0, 1)),
            pl.BlockSpec((1, S, DIFF_WIDTH), lambda b, i: (b, 0, 2)),
        ],
        out_specs=pl.BlockSpec((1, TQ, DIFF_WIDTH), lambda b, i: (b, i, 0)),
        out_shape=jax.ShapeDtypeStruct((B, S, DIFF_WIDTH), BF16),
        compiler_params=pltpu.CompilerParams(
            dimension_semantics=("arbitrary", "arbitrary"),
            vmem_limit_bytes=VMEM_LIMIT),
        name="diff_attn",
    )(slopes2, lq1, lk1, lq2, lk2, subln_g, qkv, qkv, qkv)


def _stick_break_kernel(g_ref, q_ref, k_ref, v_ref, o_ref):
    i = pl.program_id(1)
    t0 = i * TQ
    npair = N_SB_HEADS // 2
    lo, hi = _half_masks(BF16)

    r = lax.broadcasted_iota(jnp.int32, (TQ, 2 * TK), 0)
    c = lax.broadcasted_iota(jnp.int32, (TQ, 2 * TK), 1) % TK
    ju = lax.broadcasted_iota(jnp.int32, (2 * TK, 2 * TK), 0)
    su = lax.broadcasted_iota(jnp.int32, (2 * TK, 2 * TK), 1)
    ubd = jnp.where((ju >= su) & ((ju // TK) == (su // TK)), 1.0, 0.0).astype(BF16)

    def pair_block(p, s0, mask, carry):
        ca, cb, acc = carry
        cols = slice(p * LANES, (p + 1) * LANES)
        q = q_ref[0, :, cols]
        kb = k_ref[0, pl.ds(s0, TK), cols]
        vb = v_ref[0, pl.ds(s0, TK), cols]
        kbd = jnp.concatenate([kb * lo, kb * hi], axis=0)
        z = _nt_dot(q, kbd)
        sp = jnp.maximum(z, 0.0) + jnp.log2(1.0 + jnp.exp2(-jnp.abs(z)))
        if mask is not None:
            sp = jnp.where(mask, sp, 0.0)
        suf = jnp.dot(sp.astype(BF16), ubd, preferred_element_type=F32)
        cpair = jnp.concatenate([jnp.broadcast_to(ca, (TQ, TK)),
                                 jnp.broadcast_to(cb, (TQ, TK))], axis=1)
        a = jnp.exp2(z - suf - cpair)
        if mask is not None:
            a = jnp.where(mask, a, 0.0)
        vbd = jnp.concatenate([vb * lo, vb * hi], axis=0)
        acc = acc + jnp.dot(a.astype(BF16), vbd, preferred_element_type=F32)
        ca = ca + jnp.sum(sp[:, :TK], axis=1, keepdims=True)
        cb = cb + jnp.sum(sp[:, TK:], axis=1, keepdims=True)
        return ca, cb, acc

    def block(s0, mask, carry):
        return tuple(pair_block(p, s0, mask, carry[p]) for p in range(npair))

    init = tuple((jnp.zeros((TQ, 1), F32), jnp.zeros((TQ, 1), F32), jnp.zeros((TQ, LANES), F32))
                 for _ in range(npair))
    carry = block(pl.multiple_of(t0 + TK, TK), (c + TK) < r, init)
    carry = block(pl.multiple_of(t0, TK), c < r, carry)

    def body(it, carry):
        j = 2 * i - 1 - it
        return block(pl.multiple_of(j * TK, TK), None, carry)

    carry = lax.fori_loop(0, 2 * i, body, carry)
    lane = lax.broadcasted_iota(jnp.int32, (TQ, LANES), 1)
    is_a = lane < HEAD_DIM
    for p in range(npair):
        o = carry[p][2]
        o2 = o * o
        msa = jnp.sum(jnp.where(is_a, o2, 0.0), axis=1, keepdims=True) * (1.0 / HEAD_DIM)
        msb = jnp.sum(jnp.where(is_a, 0.0, o2), axis=1, keepdims=True) * (1.0 / HEAD_DIM)
        ms = jnp.where(is_a, msa, msb)
        o_ref[0, :, p * LANES:(p + 1) * LANES] = (o * lax.rsqrt(ms + EPS) * g_ref[...]).astype(BF16)


def _stick_breaking(qkv, g_pair):
    B, S, _ = qkv.shape
    qcol = 3 * DIFF_WIDTH // SB_WIDTH
    return pl.pallas_call(
        _stick_break_kernel,
        grid=(B, S // TQ),
        in_specs=[
            pl.BlockSpec((1, LANES), lambda b, i: (0, 0)),
            pl.BlockSpec((1, TQ, SB_WIDTH), lambda b, i: (b, i, qcol)),
            pl.BlockSpec((1, S, SB_WIDTH), lambda b, i: (b, 0, qcol + 1)),
            pl.BlockSpec((1, S, SB_WIDTH), lambda b, i: (b, 0, qcol + 2)),
        ],
        out_specs=pl.BlockSpec((1, TQ, SB_WIDTH), lambda b, i: (b, i, 0)),
        out_shape=jax.ShapeDtypeStruct((B, S, SB_WIDTH), BF16),
        compiler_params=pltpu.CompilerParams(
            dimension_semantics=("arbitrary", "arbitrary"),
            vmem_limit_bytes=VMEM_LIMIT),
        name="stick_break",
    )(g_pair, qkv, qkv, qkv)


def _out_ffn_kernel(x_ref, od_ref, os_ref, wo_ref, g2_ref, wu_ref, wd_ref, gf_ref, o_ref):
    h = (x_ref[...]
         + jnp.dot(od_ref[...], wo_ref[:DIFF_WIDTH, :], preferred_element_type=F32)
         + jnp.dot(os_ref[...], wo_ref[DIFF_WIDTH:, :], preferred_element_type=F32))
    ms = jnp.mean(h * h, axis=-1, keepdims=True)
    n2 = (h * lax.rsqrt(ms + EPS) * g2_ref[...]).astype(BF16)
    u = jnp.dot(n2, wu_ref[...], preferred_element_type=F32)
    u = jnp.square(jnp.maximum(u, 0.0)).astype(BF16)
    h = h + jnp.dot(u, wd_ref[...], preferred_element_type=F32)
    ms = jnp.mean(h * h, axis=-1, keepdims=True)
    o_ref[...] = h * lax.rsqrt(ms + EPS) * gf_ref[...]


def _out_ffn(x2, od, osb, wo, g2, wu, wd, gf):
    rows = x2.shape[0]
    const = lambda i: (0, 0)
    return pl.pallas_call(
        _out_ffn_kernel,
        grid=(rows // TM,),
        in_specs=[
            pl.BlockSpec((TM, D_MODEL), lambda i: (i, 0)),
            pl.BlockSpec((TM, DIFF_WIDTH), lambda i: (i, 0)),
            pl.BlockSpec((TM, SB_WIDTH), lambda i: (i, 0)),
            pl.BlockSpec((D_MODEL, D_MODEL), const),
            pl.BlockSpec((1, D_MODEL), const),
            pl.BlockSpec((D_MODEL, D_FF), const),
            pl.BlockSpec((D_FF, D_MODEL), const),
            pl.BlockSpec((1, D_MODEL), const),
        ],
        out_specs=pl.BlockSpec((TM, D_MODEL), lambda i: (i, 0)),
        out_shape=jax.ShapeDtypeStruct((rows, D_MODEL), F32),
        compiler_params=pltpu.CompilerParams(
            dimension_semantics=("arbitrary",), vmem_limit_bytes=VMEM_LIMIT),
        name="out_ffn",
    )(x2, od, osb, wo, g2, wu, wd, gf)


def kernel(x, norm1_g, w_in, lambda_q1, lambda_k1, lambda_q2, lambda_k2, diff_subln_g,
           sb_norm_g, w_out, norm2_g, w_up, w_down, final_norm_g):
    B, S, D = x.shape
    x2 = x.reshape(B * S, D)

    qs = LOG2E / math.sqrt(HEAD_DIM)
    colscale = np.ones((1, IN_WIDTH), np.float32)
    colscale[0, :DIFF_WIDTH] = qs
    colscale[0, 3 * DIFF_WIDTH:3 * DIFF_WIDTH + SB_WIDTH] = qs
    slopes2 = (np.exp2(-8.0 * np.arange(1, N_DIFF_HEADS + 1) / N_DIFF_HEADS) * LOG2E).astype(np.float32)

    qkv = _norm_proj(x2, norm1_g[0][None, :], w_in[0].astype(BF16), jnp.asarray(colscale))
    qkv = qkv.reshape(B, S, IN_WIDTH)

    o_diff = _diff_attention(qkv, jnp.asarray(slopes2), lambda_q1, lambda_k1, lambda_q2,
                             lambda_k2, diff_subln_g)
    g_pair = jnp.concatenate([sb_norm_g[0], sb_norm_g[0]])[None, :]
    o_sb = _stick_breaking(qkv, g_pair)

    out = _out_ffn(x2, o_diff.reshape(B * S, DIFF_WIDTH), o_sb.reshape(B * S, SB_WIDTH),
                   w_out[0].astype(BF16), norm2_g[0][None, :], w_up[0].astype(BF16),
                   w_down[0].astype(BF16), final_norm_g[None, :])
    return out.reshape(B, S, D)
```

```python
import math

import numpy as np
import jax
import jax.numpy as jnp
from jax import lax
from jax.experimental import pallas as pl
from jax.experimental.pallas import tpu as pltpu

D_MODEL = 1024
HEAD_DIM = 64
CHUNK = 64
N_DIFF_HEADS = 4
N_SB_HEADS = 8
DIFF_V_DIM = 2 * HEAD_DIM
DIFF_WIDTH = N_DIFF_HEADS * DIFF_V_DIM
SB_WIDTH = N_SB_HEADS * HEAD_DIM
QK_WIDTH = 2 * DIFF_WIDTH + 2 * SB_WIDTH
V_WIDTH = DIFF_WIDTH + SB_WIDTH
D_FF = 4 * D_MODEL
EPS = 1e-6
NEG_INF = -1e30
LAMBDA_INIT = 0.8 - 0.6 * math.exp(-0.3 * 0)
LOG2E = 1.4426950408889634

LANES = 128
TQ = 256
TKD = 256
TKS = 128
TM = 512
VMEM_LIMIT = 56 * 1024 * 1024

F32 = jnp.float32
BF16 = jnp.bfloat16


def _nt_dot(a, b):
    return lax.dot_general(a, b, (((1,), (1,)), ((), ())), preferred_element_type=F32)


def _half_masks(dtype):
    lane = lax.broadcasted_iota(jnp.int32, (1, LANES), 1)
    lo = jnp.where(lane < HEAD_DIM, 1.0, 0.0).astype(dtype)
    hi = jnp.where(lane >= HEAD_DIM, 1.0, 0.0).astype(dtype)
    return lo, hi


def _norm_proj_kernel(x_ref, g_ref, wqk_ref, wvt_ref, cs_ref, oqk_ref, ovt_ref):
    x = x_ref[...]
    ms = jnp.mean(x * x, axis=-1, keepdims=True)
    n = (x * lax.rsqrt(ms + EPS) * g_ref[...]).astype(BF16)
    chunk = 512
    for c in range(QK_WIDTH // chunk):
        sl = slice(c * chunk, (c + 1) * chunk)
        acc = jnp.dot(n, wqk_ref[:, sl], preferred_element_type=F32)
        oqk_ref[:, sl] = (acc * cs_ref[:, sl]).astype(BF16)
    for c in range(V_WIDTH // chunk):
        sl = slice(c * chunk, (c + 1) * chunk)
        ovt_ref[sl, :] = _nt_dot(wvt_ref[sl, :], n).astype(BF16)


def _norm_proj(x2, g, wqk, wvt, colscale):
    rows = x2.shape[0]
    const = lambda i: (0, 0)
    return pl.pallas_call(
        _norm_proj_kernel,
        grid=(rows // TM,),
        in_specs=[
            pl.BlockSpec((TM, D_MODEL), lambda i: (i, 0)),
            pl.BlockSpec((1, D_MODEL), const),
            pl.BlockSpec((D_MODEL, QK_WIDTH), const),
            pl.BlockSpec((V_WIDTH, D_MODEL), const),
            pl.BlockSpec((1, QK_WIDTH), const),
        ],
        out_specs=[pl.BlockSpec((TM, QK_WIDTH), lambda i: (i, 0)),
                   pl.BlockSpec((V_WIDTH, TM), lambda i: (0, i))],
        out_shape=[jax.ShapeDtypeStruct((rows, QK_WIDTH), BF16),
                   jax.ShapeDtypeStruct((V_WIDTH, rows), BF16)],
        compiler_params=pltpu.CompilerParams(
            dimension_semantics=("arbitrary",), vmem_limit_bytes=VMEM_LIMIT),
        name="norm_proj",
    )(x2, g, wqk, wvt, colscale)


def _diff_attn_kernel(slopes_ref, lq1_ref, lk1_ref, lq2_ref, lk2_ref, g_ref,
                      q_ref, k_ref, vt_ref, o_ref):
    i = pl.program_id(1)
    t0 = i * TQ

    lam = (jnp.exp(jnp.sum(lq1_ref[...] * lk1_ref[...], axis=1, keepdims=True))
           - jnp.exp(jnp.sum(lq2_ref[...] * lk2_ref[...], axis=1, keepdims=True))
           + LAMBDA_INIT)

    lo, hi = _half_masks(BF16)
    c = lax.broadcasted_iota(jnp.int32, (TKD, TQ), 0)
    r = lax.broadcasted_iota(jnp.int32, (TKD, TQ), 1)
    rel = (r - c).astype(F32)
    vis = (c // CHUNK) <= (r // CHUNK)
    dist = jnp.abs(rel)

    def head_block(h, s0, bias, shift, carry):
        m1, m2, l1, l2, acc1, acc2 = carry
        cols = slice(h * LANES, (h + 1) * LANES)
        rows = slice(h * DIFF_V_DIM, (h + 1) * DIFF_V_DIM)
        q = q_ref[0, :, cols]
        kb = k_ref[0, pl.ds(s0, TKD), cols]
        vt = vt_ref[rows, pl.ds(s0, TKD)]
        kbd = jnp.concatenate([kb * lo, kb * hi], axis=0)
        s = _nt_dot(kbd, q)
        u1 = s[:TKD] + bias
        u2 = s[TKD:] + bias
        m1n = jnp.maximum(m1, jnp.max(u1, axis=0, keepdims=True) + shift)
        m2n = jnp.maximum(m2, jnp.max(u2, axis=0, keepdims=True) + shift)
        p1 = jnp.exp2(u1 - (m1n - shift))
        p2 = jnp.exp2(u2 - (m2n - shift))
        a1 = jnp.exp2(m1 - m1n)
        a2 = jnp.exp2(m2 - m2n)
        l1 = a1 * l1 + jnp.sum(p1, axis=0, keepdims=True)
        l2 = a2 * l2 + jnp.sum(p2, axis=0, keepdims=True)
        acc1 = acc1 * a1 + jnp.dot(vt, p1.astype(BF16), preferred_element_type=F32)
        acc2 = acc2 * a2 + jnp.dot(vt, p2.astype(BF16), preferred_element_type=F32)
        return m1n, m2n, l1, l2, acc1, acc2

    heads = range(N_DIFF_HEADS)
    stat = lambda v: jnp.full((1, TQ), v, F32)
    zacc = jnp.zeros((DIFF_V_DIM, TQ), F32)
    carry = tuple((stat(NEG_INF), stat(NEG_INF), stat(0.0), stat(0.0), zacc, zacc) for _ in heads)

    s0 = pl.multiple_of(t0, TKD)
    carry = tuple(head_block(h, s0, jnp.where(vis, -slopes_ref[h] * dist, NEG_INF), 0.0, carry[h])
                  for h in heads)

    biases = tuple(-slopes_ref[h] * rel for h in heads)

    def body(j, carry):
        s0 = pl.multiple_of(j * TKD, TKD)
        dt = (t0 - s0).astype(F32)
        return tuple(head_block(h, s0, biases[h], -slopes_ref[h] * dt, carry[h]) for h in heads)

    carry = lax.fori_loop(0, i, body, carry)
    for h in heads:
        m1, m2, l1, l2, acc1, acc2 = carry[h]
        o = acc1 / l1 - lam * (acc2 / l2)
        ms = jnp.mean(o * o, axis=0, keepdims=True)
        o = o * lax.rsqrt(ms + EPS) * g_ref[...] * (1.0 - LAMBDA_INIT)
        o_ref[0, :, h * LANES:(h + 1) * LANES] = o.T.astype(BF16)


def _diff_attention(qk3, vt, slopes2, lq1, lk1, lq2, lk2, subln_g_col):
    B, S, _ = qk3.shape
    vec = pl.BlockSpec((1, HEAD_DIM), lambda b, i: (0, 0))
    return pl.pallas_call(
        _diff_attn_kernel,
        grid=(B, S // TQ),
        in_specs=[
            pl.BlockSpec(memory_space=pltpu.SMEM),
            vec, vec, vec, vec,
            pl.BlockSpec((DIFF_V_DIM, 1), lambda b, i: (0, 0)),
            pl.BlockSpec((1, TQ, DIFF_WIDTH), lambda b, i: (b, i, 0)),
            pl.BlockSpec((1, S, DIFF_WIDTH), lambda b, i: (b, 0, 1)),
            pl.BlockSpec((DIFF_WIDTH, S), lambda b, i: (0, b)),
        ],
        out_specs=pl.BlockSpec((1, TQ, DIFF_WIDTH), lambda b, i: (b, i, 0)),
        out_shape=jax.ShapeDtypeStruct((B, S, DIFF_WIDTH), BF16),
        compiler_params=pltpu.CompilerParams(
            dimension_semantics=("arbitrary", "arbitrary"),
            vmem_limit_bytes=VMEM_LIMIT),
        name="diff_attn",
    )(slopes2, lq1, lk1, lq2, lk2, subln_g_col, qk3, qk3, vt)


def _stick_break_kernel(g_ref, q_ref, k_ref, vt_ref, o_ref):
    i = pl.program_id(1)
    t0 = i * TQ
    npair = N_SB_HEADS // 2
    lo, hi = _half_masks(BF16)

    c = lax.broadcasted_iota(jnp.int32, (2 * TKS, TQ), 0) % TKS
    r = lax.broadcasted_iota(jnp.int32, (2 * TKS, TQ), 1)
    su = lax.broadcasted_iota(jnp.int32, (2 * TKS, 2 * TKS), 0)
    ju = lax.broadcasted_iota(jnp.int32, (2 * TKS, 2 * TKS), 1)
    utbd = jnp.where((ju >= su) & ((ju // TKS) == (su // TKS)), 1.0, 0.0).astype(BF16)
    drow = lax.broadcasted_iota(jnp.int32, (LANES, TKS), 0)
    va_mask = jnp.where(drow < HEAD_DIM, 1.0, 0.0).astype(BF16)
    vb_mask = jnp.where(drow >= HEAD_DIM, 1.0, 0.0).astype(BF16)
    sign = jnp.uint32(0x80000000)

    def pair_block(p, s0, mask_bias, carry):
        ca, cb, acc = carry
        cols = slice(p * LANES, (p + 1) * LANES)
        q = q_ref[0, :, cols]
        kb = k_ref[0, pl.ds(s0, TKS), cols]
        vt = vt_ref[cols, pl.ds(s0, TKS)]
        kbd = jnp.concatenate([kb * lo, kb * hi], axis=0)
        z = _nt_dot(kbd, q)
        if mask_bias is not None:
            z = z + mask_bias
        nabs = lax.bitcast_convert_type(lax.bitcast_convert_type(z, jnp.uint32) | sign, F32)
        sp = jnp.maximum(z, 0.0) + jnp.log2(1.0 + jnp.exp2(nabs))
        suf = jnp.dot(utbd, sp.astype(BF16), preferred_element_type=F32)
        cpair = jnp.concatenate([jnp.broadcast_to(ca, (TKS, TQ)),
                                 jnp.broadcast_to(cb, (TKS, TQ))], axis=0)
        a = jnp.exp2(z - suf - cpair)
        vbd = jnp.concatenate([vt * va_mask, vt * vb_mask], axis=1)
        acc = acc + jnp.dot(vbd, a.astype(BF16), preferred_element_type=F32)
        ca = ca + suf[0:1, :]
        cb = cb + suf[TKS:TKS + 1, :]
        return ca, cb, acc

    def block(s0, mask_bias, carry):
        return tuple(pair_block(p, s0, mask_bias, carry[p]) for p in range(npair))

    zc = jnp.zeros((1, TQ), F32)
    carry = tuple((zc, zc, jnp.zeros((LANES, TQ), F32)) for _ in range(npair))
    for d in reversed(range(TQ // TKS)):
        mb = jnp.where((c + d * TKS) < r, 0.0, NEG_INF)
        carry = block(pl.multiple_of(t0 + d * TKS, TKS), mb, carry)

    nfull = (TQ // TKS) * i

    def body(it, carry):
        j = nfull - 1 - it
        return block(pl.multiple_of(j * TKS, TKS), None, carry)

    carry = lax.fori_loop(0, nfull, body, carry)
    for p in range(npair):
        o = carry[p][2]
        o2 = o * o
        msa = jnp.mean(o2[:HEAD_DIM], axis=0, keepdims=True)
        msb = jnp.mean(o2[HEAD_DIM:], axis=0, keepdims=True)
        ms = jnp.concatenate([jnp.broadcast_to(msa, (HEAD_DIM, TQ)),
                              jnp.broadcast_to(msb, (HEAD_DIM, TQ))], axis=0)
        o = o * lax.rsqrt(ms + EPS) * g_ref[...]
        o_ref[0, :, p * LANES:(p + 1) * LANES] = o.T.astype(BF16)


def _stick_breaking(qk3, vt, g_pair_col):
    B, S, _ = qk3.shape
    return pl.pallas_call(
        _stick_break_kernel,
        grid=(B, S // TQ),
        in_specs=[
            pl.BlockSpec((LANES, 1), lambda b, i: (0, 0)),
            pl.BlockSpec((1, TQ, SB_WIDTH), lambda b, i: (b, i, 2)),
            pl.BlockSpec((1, S, SB_WIDTH), lambda b, i: (b, 0, 3)),
            pl.BlockSpec((SB_WIDTH, S), lambda b, i: (1, b)),
        ],
        out_specs=pl.BlockSpec((1, TQ, SB_WIDTH), lambda b, i: (b, i, 0)),
        out_shape=jax.ShapeDtypeStruct((B, S, SB_WIDTH), BF16),
        compiler_params=pltpu.CompilerParams(
            dimension_semantics=("arbitrary", "arbitrary"),
            vmem_limit_bytes=VMEM_LIMIT),
        name="stick_break",
    )(g_pair_col, qk3, qk3, vt)


def _out_ffn_kernel(x_ref, od_ref, os_ref, wo_ref, g2_ref, wu_ref, wd_ref, gf_ref, o_ref):
    h = (x_ref[...]
         + jnp.dot(od_ref[...], wo_ref[:DIFF_WIDTH, :], preferred_element_type=F32)
         + jnp.dot(os_ref[...], wo_ref[DIFF_WIDTH:, :], preferred_element_type=F32))
    ms = jnp.mean(h * h, axis=-1, keepdims=True)
    n2 = (h * lax.rsqrt(ms + EPS) * g2_ref[...]).astype(BF16)
    u = jnp.dot(n2, wu_ref[...], preferred_element_type=F32)
    u = jnp.square(jnp.maximum(u, 0.0)).astype(BF16)
    h = h + jnp.dot(u, wd_ref[...], preferred_element_type=F32)
    ms = jnp.mean(h * h, axis=-1, keepdims=True)
    o_ref[...] = h * lax.rsqrt(ms + EPS) * gf_ref[...]


def _out_ffn(x2, od, osb, wo, g2, wu, wd, gf):
    rows = x2.shape[0]
    const = lambda i: (0, 0)
    return pl.pallas_call(
        _out_ffn_kernel,
        grid=(rows // TM,),
        in_specs=[
            pl.BlockSpec((TM, D_MODEL), lambda i: (i, 0)),
            pl.BlockSpec((TM, DIFF_WIDTH), lambda i: (i, 0)),
            pl.BlockSpec((TM, SB_WIDTH), lambda i: (i, 0)),
            pl.BlockSpec((D_MODEL, D_MODEL), const),
            pl.BlockSpec((1, D_MODEL), const),
            pl.BlockSpec((D_MODEL, D_FF), const),
            pl.BlockSpec((D_FF, D_MODEL), const),
            pl.BlockSpec((1, D_MODEL), const),
        ],
        out_specs=pl.BlockSpec((TM, D_MODEL), lambda i: (i, 0)),
        out_shape=jax.ShapeDtypeStruct((rows, D_MODEL), F32),
        compiler_params=pltpu.CompilerParams(
            dimension_semantics=("arbitrary",), vmem_limit_bytes=VMEM_LIMIT),
        name="out_ffn",
    )(x2, od, osb, wo, g2, wu, wd, gf)


def kernel(x, norm1_g, w_in, lambda_q1, lambda_k1, lambda_q2, lambda_k2, diff_subln_g,
           sb_norm_g, w_out, norm2_g, w_up, w_down, final_norm_g):
    B, S, D = x.shape
    x2 = x.reshape(B * S, D)

    qs = LOG2E / math.sqrt(HEAD_DIM)
    colscale = np.ones((1, QK_WIDTH), np.float32)
    colscale[0, :DIFF_WIDTH] = qs
    colscale[0, 2 * DIFF_WIDTH:2 * DIFF_WIDTH + SB_WIDTH] = qs
    slopes2 = (np.exp2(-8.0 * np.arange(1, N_DIFF_HEADS + 1) / N_DIFF_HEADS) * LOG2E).astype(np.float32)

    w = w_in[0]
    d3 = 3 * DIFF_WIDTH
    wqk = jnp.concatenate([w[:, :2 * DIFF_WIDTH], w[:, d3:d3 + 2 * SB_WIDTH]], axis=1).astype(BF16)
    wvt = jnp.concatenate([w[:, 2 * DIFF_WIDTH:d3], w[:, d3 + 2 * SB_WIDTH:]], axis=1).T.astype(BF16)

    qk, vt = _norm_proj(x2, norm1_g[0][None, :], wqk, wvt, jnp.asarray(colscale))
    qk3 = qk.reshape(B, S, QK_WIDTH)

    o_diff = _diff_attention(qk3, vt, jnp.asarray(slopes2), lambda_q1, lambda_k1, lambda_q2,
                             lambda_k2, diff_subln_g[0][:, None])
    g_pair = jnp.concatenate([sb_norm_g[0], sb_norm_g[0]])[:, None]
    o_sb = _stick_breaking(qk3, vt, g_pair)

    out = _out_ffn(x2, o_diff.reshape(B * S, DIFF_WIDTH), o_sb.reshape(B * S, SB_WIDTH),
                   w_out[0].astype(BF16), norm2_g[0][None, :], w_up[0].astype(BF16),
                   w_down[0].astype(BF16), final_norm_g[None, :])
    return out.reshape(B, S, D)
```

```python
import math

import numpy as np
import jax
import jax.numpy as jnp
from jax import lax
from jax.experimental import pallas as pl
from jax.experimental.pallas import tpu as pltpu

D_MODEL = 1024
HEAD_DIM = 64
CHUNK = 64
N_DIFF_HEADS = 4
N_SB_HEADS = 8
DIFF_V_DIM = 2 * HEAD_DIM
DIFF_WIDTH = N_DIFF_HEADS * DIFF_V_DIM
SB_WIDTH = N_SB_HEADS * HEAD_DIM
QK_WIDTH = 2 * DIFF_WIDTH + 2 * SB_WIDTH
V_WIDTH = DIFF_WIDTH + SB_WIDTH
D_FF = 4 * D_MODEL
EPS = 1e-6
NEG_INF = -1e30
LAMBDA_INIT = 0.8 - 0.6 * math.exp(-0.3 * 0)
LOG2E = 1.4426950408889634

LANES = 128
TQ = 256
TKD = 256
TKS = 128
TM = 512
VMEM_LIMIT = 56 * 1024 * 1024

F32 = jnp.float32
BF16 = jnp.bfloat16


def _nt_dot(a, b):
    return lax.dot_general(a, b, (((1,), (1,)), ((), ())), preferred_element_type=F32)


def _half_masks(dtype):
    lane = lax.broadcasted_iota(jnp.int32, (1, LANES), 1)
    lo = jnp.where(lane < HEAD_DIM, 1.0, 0.0).astype(dtype)
    hi = jnp.where(lane >= HEAD_DIM, 1.0, 0.0).astype(dtype)
    return lo, hi


def _norm_proj_kernel(x_ref, g_ref, wqk_ref, wvt_ref, cs_ref, oqk_ref, ovt_ref):
    x = x_ref[...]
    ms = jnp.mean(x * x, axis=-1, keepdims=True)
    n = (x * lax.rsqrt(ms + EPS) * g_ref[...]).astype(BF16)
    chunk = 512
    for c in range(QK_WIDTH // chunk):
        sl = slice(c * chunk, (c + 1) * chunk)
        acc = jnp.dot(n, wqk_ref[:, sl], preferred_element_type=F32)
        oqk_ref[:, sl] = (acc * cs_ref[:, sl]).astype(BF16)
    for c in range(V_WIDTH // chunk):
        sl = slice(c * chunk, (c + 1) * chunk)
        ovt_ref[sl, :] = _nt_dot(wvt_ref[sl, :], n).astype(BF16)


def _norm_proj(x2, g, wqk, wvt, colscale):
    rows = x2.shape[0]
    const = lambda i: (0, 0)
    return pl.pallas_call(
        _norm_proj_kernel,
        grid=(rows // TM,),
        in_specs=[
            pl.BlockSpec((TM, D_MODEL), lambda i: (i, 0)),
            pl.BlockSpec((1, D_MODEL), const),
            pl.BlockSpec((D_MODEL, QK_WIDTH), const),
            pl.BlockSpec((V_WIDTH, D_MODEL), const),
            pl.BlockSpec((1, QK_WIDTH), const),
        ],
        out_specs=[pl.BlockSpec((TM, QK_WIDTH), lambda i: (i, 0)),
                   pl.BlockSpec((V_WIDTH, TM), lambda i: (0, i))],
        out_shape=[jax.ShapeDtypeStruct((rows, QK_WIDTH), BF16),
                   jax.ShapeDtypeStruct((V_WIDTH, rows), BF16)],
        compiler_params=pltpu.CompilerParams(
            dimension_semantics=("arbitrary",), vmem_limit_bytes=VMEM_LIMIT),
        name="norm_proj",
    )(x2, g, wqk, wvt, colscale)


def _diff_schedule(seq):
    items = []
    for i in range(seq // TQ):
        t0 = i * TQ
        items.append((t0, t0, 0, 0, 0))
        for j in range(i):
            items.append((t0, j * TKD, 1, 1, 0))
        items[-1] = items[-1][:4] + (1,)
    n = len(items)
    tab = np.zeros((7, n + 1), np.int32)
    for t in range(n + 1):
        q1, k1, kind, keep, _ = items[min(t, n - 1)]
        tab[0:4, t] = (q1, k1, kind, keep if t < n else 1)
        if t >= 1:
            q2, k2, _, _, last = items[t - 1]
            tab[4:7, t] = (k2, q2, last)
    return tab


def _sb_schedule(seq):
    items = []
    nd = TQ // TKS
    for i in range(seq // TQ):
        t0 = i * TQ
        for d in reversed(range(nd)):
            items.append([t0, t0 + d * TKS, nd - 1 - d, 1, 0])
        for j in reversed(range(nd * i)):
            items.append([t0, j * TKS, nd, 1, 0])
        items[-(nd * (i + 1))][3] = 0
        items[-1][4] = 1
    n = len(items)
    tab = np.zeros((8, n + 2), np.int32)
    tab[3] = 1
    tab[5] = 1
    for t in range(n + 2):
        q1, k1, msel, _, _ = items[min(t, n - 1)]
        tab[0:3, t] = (q1, k1, msel)
        if 1 <= t <= n:
            tab[3, t] = items[t - 1][3]
        if 2 <= t:
            q3, k3, _, keep, last = items[t - 2]
            tab[4:8, t] = (k3, keep, q3, last)
    return tab


def _diff_attn_kernel(tab_ref, slopes_ref, lq1_ref, lk1_ref, lq2_ref, lk2_ref, g_ref,
                      bias_ref, q_ref, k_ref, vt_ref, o_ref,
                      p_scr, acc_scr, alpha_scr, m_scr, l_scr, lfin_scr):
    nsteps = tab_ref.shape[1]
    heads = range(N_DIFF_HEADS)
    lo, hi = _half_masks(BF16)
    lam = (jnp.exp(jnp.sum(lq1_ref[...] * lk1_ref[...], axis=1, keepdims=True))
           - jnp.exp(jnp.sum(lq2_ref[...] * lk2_ref[...], axis=1, keepdims=True))
           + LAMBDA_INIT)

    p_scr[...] = jnp.zeros(p_scr.shape, BF16)
    acc_scr[...] = jnp.zeros(acc_scr.shape, F32)
    alpha_scr[...] = jnp.zeros(alpha_scr.shape, F32)
    m_scr[...] = jnp.full(m_scr.shape, NEG_INF, F32)
    l_scr[...] = jnp.zeros(l_scr.shape, F32)
    lfin_scr[...] = jnp.ones(lfin_scr.shape, F32)

    def step(t, _):
        ks2 = pl.multiple_of(tab_ref[4, t], TKD)
        for h in heads:
            vt = vt_ref[h * DIFF_V_DIM:(h + 1) * DIFF_V_DIM, pl.ds(ks2, TKD)]
            for m in range(2):
                row = 2 * h + m
                pv = jnp.dot(vt, p_scr[row], preferred_element_type=F32)
                acc_scr[row] = acc_scr[row] * alpha_scr[row:row + 1, :] + pv

        qs1 = pl.multiple_of(tab_ref[0, t], TQ)
        ks1 = pl.multiple_of(tab_ref[1, t], TKD)
        kind = tab_ref[2, t]
        keep = tab_ref[3, t].astype(F32)
        dt = (qs1 - ks1).astype(F32)
        for h in heads:
            cols = slice(h * LANES, (h + 1) * LANES)
            q = q_ref[0, pl.ds(qs1, TQ), cols]
            kb = k_ref[0, pl.ds(ks1, TKD), cols]
            kbd = jnp.concatenate([kb * lo, kb * hi], axis=0)
            s = _nt_dot(kbd, q)
            bias = bias_ref[h, kind]
            shift = -slopes_ref[h] * dt
            for m in range(2):
                row = 2 * h + m
                u = s[m * TKD:(m + 1) * TKD] + bias
                m_prev = m_scr[row:row + 1, :]
                l_prev = l_scr[row:row + 1, :]
                lfin_scr[row:row + 1, :] = l_prev
                m_old = m_prev * keep + NEG_INF * (1.0 - keep)
                m_new = jnp.maximum(m_old, jnp.max(u, axis=0, keepdims=True) + shift)
                p = jnp.exp2(u - (m_new - shift))
                alpha = jnp.exp2(m_old - m_new)
                l_scr[row:row + 1, :] = alpha * (l_prev * keep) + jnp.sum(p, axis=0, keepdims=True)
                m_scr[row:row + 1, :] = m_new
                alpha_scr[row:row + 1, :] = alpha
                p_scr[row] = p.astype(BF16)

        @pl.when(tab_ref[6, t] == 1)
        def _():
            qs2 = pl.multiple_of(tab_ref[5, t], TQ)
            for h in heads:
                o = (acc_scr[2 * h] / lfin_scr[2 * h:2 * h + 1, :]
                     - lam * (acc_scr[2 * h + 1] / lfin_scr[2 * h + 1:2 * h + 2, :]))
                ms = jnp.mean(o * o, axis=0, keepdims=True)
                o = o * lax.rsqrt(ms + EPS) * g_ref[...] * (1.0 - LAMBDA_INIT)
                o_ref[0, pl.ds(qs2, TQ), h * LANES:(h + 1) * LANES] = o.T.astype(BF16)
        return 0

    lax.fori_loop(0, nsteps, step, 0)


def _diff_attention(qk3, vt, slopes2, lq1, lk1, lq2, lk2, subln_g_col):
    B, S, _ = qk3.shape
    tab = _diff_schedule(S)
    c = np.arange(TKD)[:, None]
    r = np.arange(TQ)[None, :]
    slopes = np.asarray(slopes2, np.float64)[:, None, None]
    diag = np.where((c // CHUNK) <= (r // CHUNK), -slopes * np.abs(r - c), NEG_INF)
    full = -slopes * (r - c)
    bias = np.stack([diag, full], axis=1).astype(np.float32)
    vec = pl.BlockSpec((1, HEAD_DIM), lambda b, tab: (0, 0))
    grid_spec = pltpu.PrefetchScalarGridSpec(
        num_scalar_prefetch=1,
        grid=(B,),
        in_specs=[
            pl.BlockSpec(memory_space=pltpu.SMEM),
            vec, vec, vec, vec,
            pl.BlockSpec((DIFF_V_DIM, 1), lambda b, tab: (0, 0)),
            pl.BlockSpec((N_DIFF_HEADS, 2, TKD, TQ), lambda b, tab: (0, 0, 0, 0)),
            pl.BlockSpec((1, S, DIFF_WIDTH), lambda b, tab: (b, 0, 0)),
            pl.BlockSpec((1, S, DIFF_WIDTH), lambda b, tab: (b, 0, 1)),
            pl.BlockSpec((DIFF_WIDTH, S), lambda b, tab: (0, b)),
        ],
        out_specs=pl.BlockSpec((1, S, DIFF_WIDTH), lambda b, tab: (b, 0, 0)),
        scratch_shapes=[
            pltpu.VMEM((2 * N_DIFF_HEADS, TKD, TQ), BF16),
            pltpu.VMEM((2 * N_DIFF_HEADS, DIFF_V_DIM, TQ), F32),
            pltpu.VMEM((2 * N_DIFF_HEADS, TQ), F32),
            pltpu.VMEM((2 * N_DIFF_HEADS, TQ), F32),
            pltpu.VMEM((2 * N_DIFF_HEADS, TQ), F32),
            pltpu.VMEM((2 * N_DIFF_HEADS, TQ), F32),
        ],
    )
    return pl.pallas_call(
        _diff_attn_kernel,
        grid_spec=grid_spec,
        out_shape=jax.ShapeDtypeStruct((B, S, DIFF_WIDTH), BF16),
        compiler_params=pltpu.CompilerParams(
            dimension_semantics=("arbitrary",), vmem_limit_bytes=VMEM_LIMIT),
        name="diff_attn",
    )(jnp.asarray(tab), slopes2, lq1, lk1, lq2, lk2, subln_g_col, jnp.asarray(bias), qk3, qk3, vt)


def _stick_break_kernel(tab_ref, g_ref, mask_ref, q_ref, k_ref, vt_ref, o_ref,
                        z_scr, sp_scr, a_scr, acc_scr, c_scr):
    nsteps = tab_ref.shape[1]
    pairs = range(N_SB_HEADS // 2)
    lo, hi = _half_masks(BF16)
    su = lax.broadcasted_iota(jnp.int32, (2 * TKS, 2 * TKS), 0)
    ju = lax.broadcasted_iota(jnp.int32, (2 * TKS, 2 * TKS), 1)
    utbd = jnp.where((ju >= su) & ((ju // TKS) == (su // TKS)), 1.0, 0.0).astype(BF16)
    drow = lax.broadcasted_iota(jnp.int32, (LANES, TKS), 0)
    va_mask = jnp.where(drow < HEAD_DIM, 1.0, 0.0).astype(BF16)
    vb_mask = jnp.where(drow >= HEAD_DIM, 1.0, 0.0).astype(BF16)

    z_scr[...] = jnp.zeros(z_scr.shape, F32)
    sp_scr[...] = jnp.zeros(sp_scr.shape, BF16)
    a_scr[...] = jnp.zeros(a_scr.shape, BF16)
    acc_scr[...] = jnp.zeros(acc_scr.shape, F32)
    c_scr[...] = jnp.zeros(c_scr.shape, F32)

    def step(t, _):
        ks3 = pl.multiple_of(tab_ref[4, t], TKS)
        keep3 = tab_ref[5, t].astype(F32)
        for p in pairs:
            vt = vt_ref[p * LANES:(p + 1) * LANES, pl.ds(ks3, TKS)]
            vbd = jnp.concatenate([vt * va_mask, vt * vb_mask], axis=1)
            acc_scr[p] = acc_scr[p] * keep3 + jnp.dot(vbd, a_scr[p], preferred_element_type=F32)

        keep2 = tab_ref[3, t].astype(F32)
        for p in pairs:
            suf = jnp.dot(utbd, sp_scr[p], preferred_element_type=F32)
            ca = c_scr[2 * p:2 * p + 1, :] * keep2
            cb = c_scr[2 * p + 1:2 * p + 2, :] * keep2
            cpair = jnp.concatenate([jnp.broadcast_to(ca, (TKS, TQ)),
                                     jnp.broadcast_to(cb, (TKS, TQ))], axis=0)
            a_scr[p] = jnp.exp2(z_scr[p] - suf - cpair).astype(BF16)
            c_scr[2 * p:2 * p + 1, :] = ca + suf[0:1, :]
            c_scr[2 * p + 1:2 * p + 2, :] = cb + suf[TKS:TKS + 1, :]

        qs1 = pl.multiple_of(tab_ref[0, t], TQ)
        ks1 = pl.multiple_of(tab_ref[1, t], TKS)
        mask_bias = mask_ref[tab_ref[2, t]]
        for p in pairs:
            cols = slice(p * LANES, (p + 1) * LANES)
            q = q_ref[0, pl.ds(qs1, TQ), cols]
            kb = k_ref[0, pl.ds(ks1, TKS), cols]
            kbd = jnp.concatenate([kb * lo, kb * hi], axis=0)
            z = _nt_dot(kbd, q) + mask_bias
            sp = jnp.maximum(z, 0.0) + jnp.log2(1.0 + jnp.exp2(-jnp.abs(z)))
            z_scr[p] = z
            sp_scr[p] = sp.astype(BF16)

        @pl.when(tab_ref[7, t] == 1)
        def _():
            qs3 = pl.multiple_of(tab_ref[6, t], TQ)
            for p in pairs:
                o = acc_scr[p]
                o2 = o * o
                msa = jnp.mean(o2[:HEAD_DIM], axis=0, keepdims=True)
                msb = jnp.mean(o2[HEAD_DIM:], axis=0, keepdims=True)
                ms = jnp.concatenate([jnp.broadcast_to(msa, (HEAD_DIM, TQ)),
                                      jnp.broadcast_to(msb, (HEAD_DIM, TQ))], axis=0)
                o = o * lax.rsqrt(ms + EPS) * g_ref[...]
                o_ref[0, pl.ds(qs3, TQ), p * LANES:(p + 1) * LANES] = o.T.astype(BF16)
        return 0

    lax.fori_loop(0, nsteps, step, 0)


def _stick_breaking(qk3, vt, g_pair_col):
    B, S, _ = qk3.shape
    tab = _sb_schedule(S)
    nd = TQ // TKS
    c = (np.arange(2 * TKS) % TKS)[:, None]
    r = np.arange(TQ)[None, :]
    mask = np.stack([np.where(c + d * TKS < r, 0.0, NEG_INF) for d in reversed(range(nd))]
                    + [np.zeros((2 * TKS, TQ))]).astype(np.float32)
    npair = N_SB_HEADS // 2
    grid_spec = pltpu.PrefetchScalarGridSpec(
        num_scalar_prefetch=1,
        grid=(B,),
        in_specs=[
            pl.BlockSpec((LANES, 1), lambda b, tab: (0, 0)),
            pl.BlockSpec((nd + 1, 2 * TKS, TQ), lambda b, tab: (0, 0, 0)),
            pl.BlockSpec((1, S, SB_WIDTH), lambda b, tab: (b, 0, 2)),
            pl.BlockSpec((1, S, SB_WIDTH), lambda b, tab: (b, 0, 3)),
            pl.BlockSpec((SB_WIDTH, S), lambda b, tab: (1, b)),
        ],
        out_specs=pl.BlockSpec((1, S, SB_WIDTH), lambda b, tab: (b, 0, 0)),
        scratch_shapes=[
            pltpu.VMEM((npair, 2 * TKS, TQ), F32),
            pltpu.VMEM((npair, 2 * TKS, TQ), BF16),
            pltpu.VMEM((npair, 2 * TKS, TQ), BF16),
            pltpu.VMEM((npair, LANES, TQ), F32),
            pltpu.VMEM((2 * npair, TQ), F32),
        ],
    )
    return pl.pallas_call(
        _stick_break_kernel,
        grid_spec=grid_spec,
        out_shape=jax.ShapeDtypeStruct((B, S, SB_WIDTH), BF16),
        compiler_params=pltpu.CompilerParams(
            dimension_semantics=("arbitrary",), vmem_limit_bytes=VMEM_LIMIT),
        name="stick_break",
    )(jnp.asarray(tab), g_pair_col, jnp.asarray(mask), qk3, qk3, vt)


def _out_ffn_kernel(x_ref, od_ref, os_ref, wo_ref, g2_ref, wu_ref, wd_ref, gf_ref, o_ref):
    h = (x_ref[...]
         + jnp.dot(od_ref[...], wo_ref[:DIFF_WIDTH, :], preferred_element_type=F32)
         + jnp.dot(os_ref[...], wo_ref[DIFF_WIDTH:, :], preferred_element_type=F32))
    ms = jnp.mean(h * h, axis=-1, keepdims=True)
    n2 = (h * lax.rsqrt(ms + EPS) * g2_ref[...]).astype(BF16)
    u = jnp.dot(n2, wu_ref[...], preferred_element_type=F32)
    u = jnp.square(jnp.maximum(u, 0.0)).astype(BF16)
    h = h + jnp.dot(u, wd_ref[...], preferred_element_type=F32)
    ms = jnp.mean(h * h, axis=-1, keepdims=True)
    o_ref[...] = h * lax.rsqrt(ms + EPS) * gf_ref[...]


def _out_ffn(x2, od, osb, wo, g2, wu, wd, gf):
    rows = x2.shape[0]
    const = lambda i: (0, 0)
    return pl.pallas_call(
        _out_ffn_kernel,
        grid=(rows // TM,),
        in_specs=[
            pl.BlockSpec((TM, D_MODEL), lambda i: (i, 0)),
            pl.BlockSpec((TM, DIFF_WIDTH), lambda i: (i, 0)),
            pl.BlockSpec((TM, SB_WIDTH), lambda i: (i, 0)),
            pl.BlockSpec((D_MODEL, D_MODEL), const),
            pl.BlockSpec((1, D_MODEL), const),
            pl.BlockSpec((D_MODEL, D_FF), const),
            pl.BlockSpec((D_FF, D_MODEL), const),
            pl.BlockSpec((1, D_MODEL), const),
        ],
        out_specs=pl.BlockSpec((TM, D_MODEL), lambda i: (i, 0)),
        out_shape=jax.ShapeDtypeStruct((rows, D_MODEL), F32),
        compiler_params=pltpu.CompilerParams(
            dimension_semantics=("arbitrary",), vmem_limit_bytes=VMEM_LIMIT),
        name="out_ffn",
    )(x2, od, osb, wo, g2, wu, wd, gf)


def kernel(x, norm1_g, w_in, lambda_q1, lambda_k1, lambda_q2, lambda_k2, diff_subln_g,
           sb_norm_g, w_out, norm2_g, w_up, w_down, final_norm_g):
    B, S, D = x.shape
    x2 = x.reshape(B * S, D)

    qs = LOG2E / math.sqrt(HEAD_DIM)
    colscale = np.ones((1, QK_WIDTH), np.float32)
    colscale[0, :DIFF_WIDTH] = qs
    colscale[0, 2 * DIFF_WIDTH:2 * DIFF_WIDTH + SB_WIDTH] = qs
    slopes2 = (np.exp2(-8.0 * np.arange(1, N_DIFF_HEADS + 1) / N_DIFF_HEADS) * LOG2E).astype(np.float32)

    w = w_in[0]
    d3 = 3 * DIFF_WIDTH
    wqk = jnp.concatenate([w[:, :2 * DIFF_WIDTH], w[:, d3:d3 + 2 * SB_WIDTH]], axis=1).astype(BF16)
    wvt = jnp.concatenate([w[:, 2 * DIFF_WIDTH:d3], w[:, d3 + 2 * SB_WIDTH:]], axis=1).T.astype(BF16)

    qk, vt = _norm_proj(x2, norm1_g[0][None, :], wqk, wvt, jnp.asarray(colscale))
    qk3 = qk.reshape(B, S, QK_WIDTH)

    o_diff = _diff_attention(qk3, vt, slopes2, lambda_q1, lambda_k1, lambda_q2,
                             lambda_k2, diff_subln_g[0][:, None])
    g_pair = jnp.concatenate([sb_norm_g[0], sb_norm_g[0]])[:, None]
    o_sb = _stick_breaking(qk3, vt, g_pair)

    out = _out_ffn(x2, o_diff.reshape(B * S, DIFF_WIDTH), o_sb.reshape(B * S, SB_WIDTH),
                   w_out[0].astype(BF16), norm2_g[0][None, :], w_up[0].astype(BF16),
                   w_down[0].astype(BF16), final_norm_g[None, :])
    return out.reshape(B, S, D)
```

```python
import math

import numpy as np
import jax
import jax.numpy as jnp
from jax import lax
from jax.experimental import pallas as pl
from jax.experimental.pallas import tpu as pltpu

D_MODEL = 1024
HEAD_DIM = 64
CHUNK = 64
N_DIFF_HEADS = 4
N_SB_HEADS = 8
N_SB_PAIRS = N_SB_HEADS // 2
DIFF_V_DIM = 2 * HEAD_DIM
DIFF_WIDTH = N_DIFF_HEADS * DIFF_V_DIM
SB_WIDTH = N_SB_HEADS * HEAD_DIM
Q_WIDTH = DIFF_WIDTH + SB_WIDTH
D_FF = 4 * D_MODEL
EPS = 1e-6
NEG_INF = -1e30
LAMBDA_INIT = 0.8 - 0.6 * math.exp(-0.3 * 0)
LOG2E = 1.4426950408889634

LANES = 128
BF16_ROWS = 16
TQ = 256
TKD = 256
TKS = 128
TM = 512
VD_ROWS = DIFF_V_DIM + BF16_ROWS
VMEM_LIMIT = 56 * 1024 * 1024

F32 = jnp.float32
BF16 = jnp.bfloat16


def _nt_dot(a, b):
    return lax.dot_general(a, b, (((1,), (1,)), ((), ())), preferred_element_type=F32)


def _norm_proj_kernel(x_ref, g_ref, wq_ref, wk_ref, wvt_ref, oq_ref, oklo_ref, okhi_ref,
                      ovd_ref, ovs_ref):
    x = x_ref[...]
    ms = jnp.mean(x * x, axis=-1, keepdims=True)
    n = (x * lax.rsqrt(ms + EPS) * g_ref[...]).astype(BF16)
    chunk = 512
    lane = lax.broadcasted_iota(jnp.int32, (1, chunk), 1) % LANES
    lo = jnp.where(lane < HEAD_DIM, 1.0, 0.0)
    hi = 1.0 - lo
    qscale = LOG2E / math.sqrt(HEAD_DIM)
    for c in range(Q_WIDTH // chunk):
        sl = slice(c * chunk, (c + 1) * chunk)
        oq_ref[:, sl] = (jnp.dot(n, wq_ref[:, sl], preferred_element_type=F32) * qscale).astype(BF16)
        kk = jnp.dot(n, wk_ref[:, sl], preferred_element_type=F32)
        oklo_ref[:, sl] = (kk * lo).astype(BF16)
        okhi_ref[:, sl] = (kk * hi).astype(BF16)
    vd = _nt_dot(wvt_ref[:DIFF_WIDTH, :], n)
    prow = lax.broadcasted_iota(jnp.int32, (BF16_ROWS, TM), 0)
    ones_rows = jnp.where(prow == 0, 1.0, 0.0).astype(BF16)
    for h in range(N_DIFF_HEADS):
        ovd_ref[h * VD_ROWS:h * VD_ROWS + DIFF_V_DIM, :] = (
            vd[h * DIFF_V_DIM:(h + 1) * DIFF_V_DIM].astype(BF16))
        ovd_ref[h * VD_ROWS + DIFF_V_DIM:(h + 1) * VD_ROWS, :] = ones_rows
    vs = _nt_dot(wvt_ref[DIFF_WIDTH:, :], n)
    row = lax.broadcasted_iota(jnp.int32, (SB_WIDTH, 1), 0) % LANES
    ra = jnp.where(row < HEAD_DIM, 1.0, 0.0)
    ovs_ref[:SB_WIDTH, :] = (vs * ra).astype(BF16)
    ovs_ref[SB_WIDTH:, :] = (vs * (1.0 - ra)).astype(BF16)


def _norm_proj(x2, g, wq, wk, wvt):
    rows = x2.shape[0]
    const = lambda i: (0, 0)
    row_blk = lambda w: pl.BlockSpec((TM, w), lambda i: (i, 0))
    col_blk = lambda r: pl.BlockSpec((r, TM), lambda i: (0, i))
    return pl.pallas_call(
        _norm_proj_kernel,
        grid=(rows // TM,),
        in_specs=[
            row_blk(D_MODEL),
            pl.BlockSpec((1, D_MODEL), const),
            pl.BlockSpec((D_MODEL, Q_WIDTH), const),
            pl.BlockSpec((D_MODEL, Q_WIDTH), const),
            pl.BlockSpec((Q_WIDTH, D_MODEL), const),
        ],
        out_specs=[row_blk(Q_WIDTH), row_blk(Q_WIDTH), row_blk(Q_WIDTH),
                   col_blk(N_DIFF_HEADS * VD_ROWS), col_blk(2 * SB_WIDTH)],
        out_shape=[jax.ShapeDtypeStruct((rows, Q_WIDTH), BF16),
                   jax.ShapeDtypeStruct((rows, Q_WIDTH), BF16),
                   jax.ShapeDtypeStruct((rows, Q_WIDTH), BF16),
                   jax.ShapeDtypeStruct((N_DIFF_HEADS * VD_ROWS, rows), BF16),
                   jax.ShapeDtypeStruct((2 * SB_WIDTH, rows), BF16)],
        compiler_params=pltpu.CompilerParams(
            dimension_semantics=("arbitrary",), vmem_limit_bytes=VMEM_LIMIT),
        name="norm_proj",
    )(x2, g, wq, wk, wvt)


def _stage_table(items, lags, fields, defaults):
    n = len(items)
    nsteps = n + max(lags)
    nsteps += nsteps % 2
    rows = [(s, f) for s in range(len(lags)) for f in fields[s]]
    tab = np.zeros((len(rows), nsteps), np.int32)
    for ri, (s, f) in enumerate(rows):
        for t in range(nsteps):
            idx = t - lags[s]
            if 0 <= idx < n:
                tab[ri, t] = items[idx][f]
            elif f in defaults:
                tab[ri, t] = defaults[f]
            else:
                tab[ri, t] = items[min(max(idx, 0), n - 1)][f]
    return tab, {(s, f): ri for ri, (s, f) in enumerate(rows)}


def _diff_schedule(seq):
    items = []
    for i in range(seq // TQ):
        t0 = i * TQ
        items.append(dict(q=t0, k=t0, kind=0, keep=0, dt=0, last=0))
        for j in range(i):
            items.append(dict(q=t0, k=j * TKD, kind=1, keep=1, dt=t0 - j * TKD, last=0))
        items[-1]["last"] = 1
    return _stage_table(items, lags=(0, 1, 2),
                        fields=(("q", "k"), ("kind", "keep", "dt"), ("k", "q", "last")),
                        defaults=dict(keep=1, last=0))


def _sb_schedule(seq):
    items = []
    nd = TQ // TKS
    for i in range(seq // TQ):
        t0 = i * TQ
        first = len(items)
        for d in reversed(range(nd)):
            items.append([t0, t0 + d * TKS, nd - 1 - d, 1, i, 0])
        for j in reversed(range(nd * i)):
            items.append([t0, j * TKS, nd, 1, i, 0])
        items[first][3] = 0
        for it in items[first:]:
            it[5] = len(items)
    return np.asarray(items, np.int32).T.copy()


def _make_diff_kernel(rix):
    heads = range(N_DIFF_HEADS)

    def kernel(tab_ref, slopes_ref, lq1_ref, lk1_ref, lq2_ref, lk2_ref, g_ref, bias_ref,
               q_ref, klo_ref, khi_ref, vd_ref, o_ref,
               s0_scr, s1_scr, p0_scr, p1_scr, al0_scr, al1_scr, m_scr, acc_scr):
        s_scr, p_scr, alpha_scr = (s0_scr, s1_scr), (p0_scr, p1_scr), (al0_scr, al1_scr)
        lam = (jnp.exp(jnp.sum(lq1_ref[...] * lk1_ref[...], axis=1, keepdims=True))
               - jnp.exp(jnp.sum(lq2_ref[...] * lk2_ref[...], axis=1, keepdims=True))
               + LAMBDA_INIT)

        for ref in s_scr + p_scr + alpha_scr:
            ref[...] = jnp.zeros(ref.shape, ref.dtype)
        m_scr[...] = jnp.full(m_scr.shape, NEG_INF, F32)
        acc_scr[...] = jnp.zeros(acc_scr.shape, F32)

        def step(t, cur):
            prev = 1 - cur
            qs = pl.multiple_of(tab_ref[rix[0, "q"], t], TQ)
            ks = pl.multiple_of(tab_ref[rix[0, "k"], t], TKD)
            for h in heads:
                cols = slice(h * LANES, (h + 1) * LANES)
                q = q_ref[0, pl.ds(qs, TQ), cols]
                kbd = jnp.concatenate([klo_ref[0, pl.ds(ks, TKD), cols],
                                       khi_ref[0, pl.ds(ks, TKD), cols]], axis=0)
                s_scr[cur][h] = _nt_dot(kbd, q)

            ks = pl.multiple_of(tab_ref[rix[2, "k"], t], TKD)
            for h in heads:
                vt = vd_ref[h * VD_ROWS:(h + 1) * VD_ROWS, pl.ds(ks, TKD)]
                for m in range(2):
                    row = 2 * h + m
                    pv = jnp.dot(vt, p_scr[prev][row], preferred_element_type=F32)
                    acc_scr[row] = acc_scr[row] * alpha_scr[prev][row:row + 1, :] + pv

            kind = tab_ref[rix[1, "kind"], t]
            keep = tab_ref[rix[1, "keep"], t].astype(F32)
            dt = tab_ref[rix[1, "dt"], t].astype(F32)
            for h in heads:
                bias = bias_ref[h, kind]
                shift = -slopes_ref[h] * dt
                for m in range(2):
                    row = 2 * h + m
                    u = s_scr[prev][h, m * TKD:(m + 1) * TKD, :] + bias
                    m_old = m_scr[row:row + 1, :] * keep + NEG_INF * (1.0 - keep)
                    m_new = jnp.maximum(m_old, jnp.max(u, axis=0, keepdims=True) + shift)
                    p_scr[cur][row] = jnp.exp2(u - (m_new - shift)).astype(BF16)
                    alpha_scr[cur][row:row + 1, :] = jnp.exp2(m_old - m_new)
                    m_scr[row:row + 1, :] = m_new

            @pl.when(tab_ref[rix[2, "last"], t] == 1)
            def _():
                qf = pl.multiple_of(tab_ref[rix[2, "q"], t], TQ)
                for h in heads:
                    a1 = acc_scr[2 * h]
                    a2 = acc_scr[2 * h + 1]
                    o = (a1[:DIFF_V_DIM] / a1[DIFF_V_DIM:DIFF_V_DIM + 1]
                         - lam * (a2[:DIFF_V_DIM] / a2[DIFF_V_DIM:DIFF_V_DIM + 1]))
                    ms = jnp.mean(o * o, axis=0, keepdims=True)
                    o = o * lax.rsqrt(ms + EPS) * g_ref[...] * (1.0 - LAMBDA_INIT)
                    o_ref[0, pl.ds(qf, TQ), h * LANES:(h + 1) * LANES] = o.T.astype(BF16)

        def two_steps(i, carry):
            step(2 * i, 0)
            step(2 * i + 1, 1)
            return carry

        lax.fori_loop(0, tab_ref.shape[1] // 2, two_steps, 0)

    return kernel


def _diff_attention(q3, klo3, khi3, vd, slopes2, lq1, lk1, lq2, lk2, subln_g_col):
    B, S, _ = q3.shape
    tab, rix = _diff_schedule(S)
    c = np.arange(TKD)[:, None]
    r = np.arange(TQ)[None, :]
    slopes = np.asarray(slopes2, np.float64)[:, None, None]
    diag = np.where((c // CHUNK) <= (r // CHUNK), -slopes * np.abs(r - c), NEG_INF)
    full = -slopes * (r - c)
    bias = np.stack([diag, full], axis=1).astype(np.float32)
    vec = pl.BlockSpec((1, HEAD_DIM), lambda b, tab: (0, 0))
    seq_blk = pl.BlockSpec((1, S, DIFF_WIDTH), lambda b, tab: (b, 0, 0))
    nmap = 2 * N_DIFF_HEADS
    grid_spec = pltpu.PrefetchScalarGridSpec(
        num_scalar_prefetch=1,
        grid=(B,),
        in_specs=[
            pl.BlockSpec(memory_space=pltpu.SMEM),
            vec, vec, vec, vec,
            pl.BlockSpec((DIFF_V_DIM, 1), lambda b, tab: (0, 0)),
            pl.BlockSpec((N_DIFF_HEADS, 2, TKD, TQ), lambda b, tab: (0, 0, 0, 0)),
            seq_blk, seq_blk, seq_blk,
            pl.BlockSpec((N_DIFF_HEADS * VD_ROWS, S), lambda b, tab: (0, b)),
        ],
        out_specs=seq_blk,
        scratch_shapes=[
            pltpu.VMEM((N_DIFF_HEADS, 2 * TKD, TQ), F32),
            pltpu.VMEM((N_DIFF_HEADS, 2 * TKD, TQ), F32),
            pltpu.VMEM((nmap, TKD, TQ), BF16),
            pltpu.VMEM((nmap, TKD, TQ), BF16),
            pltpu.VMEM((nmap, TQ), F32),
            pltpu.VMEM((nmap, TQ), F32),
            pltpu.VMEM((nmap, TQ), F32),
            pltpu.VMEM((nmap, VD_ROWS, TQ), F32),
        ],
    )
    return pl.pallas_call(
        _make_diff_kernel(rix),
        grid_spec=grid_spec,
        out_shape=jax.ShapeDtypeStruct((B, S, DIFF_WIDTH), BF16),
        compiler_params=pltpu.CompilerParams(
            dimension_semantics=("arbitrary",), vmem_limit_bytes=VMEM_LIMIT),
        name="diff_attn",
    )(jnp.asarray(tab), slopes2, lq1, lk1, lq2, lk2, subln_g_col, jnp.asarray(bias),
      q3, klo3, khi3, vd)


SB_Q, SB_K, SB_MSEL, SB_KEEP, SB_BLK, SB_NEXT = range(6)
SB_DEAD = 160.0


def _stick_break_kernel(tab_ref, g_ref, mask_ref, q_ref, klo_ref, khi_ref, vs_ref, o_ref,
                        z_scr, sp_scr, a_scr, acc_scr, c_scr):
    n_items = tab_ref.shape[1]
    pairs = range(N_SB_PAIRS)
    su = lax.broadcasted_iota(jnp.int32, (2 * TKS, 2 * TKS), 0)
    ju = lax.broadcasted_iota(jnp.int32, (2 * TKS, 2 * TKS), 1)
    utbd = jnp.where((ju >= su) & ((ju // TKS) == (su // TKS)), 1.0, 0.0).astype(BF16)

    z_scr[...] = jnp.full(z_scr.shape, NEG_INF, F32)
    sp_scr[...] = jnp.zeros(sp_scr.shape, BF16)
    a_scr[...] = jnp.zeros(a_scr.shape, BF16)
    acc_scr[...] = jnp.zeros(acc_scr.shape, F32)
    c_scr[...] = jnp.zeros(c_scr.shape, F32)

    def step(carry):
        i1, i2, i3 = carry
        j1 = jnp.minimum(i1, n_items - 1)
        j2 = jnp.minimum(i2, n_items - 1)
        j3 = jnp.minimum(i3, n_items - 1)

        ks = pl.multiple_of(tab_ref[SB_K, j3], TKS)
        for p in pairs:
            rows = slice(p * LANES, (p + 1) * LANES)
            vbd = jnp.concatenate([vs_ref[rows, pl.ds(ks, TKS)],
                                   vs_ref[SB_WIDTH + p * LANES:SB_WIDTH + (p + 1) * LANES,
                                          pl.ds(ks, TKS)]], axis=1)
            acc_scr[p] = acc_scr[p] + jnp.dot(vbd, a_scr[p], preferred_element_type=F32)

        keep2 = tab_ref[SB_KEEP, j2].astype(F32)
        cmin = None
        for p in pairs:
            suf = jnp.dot(utbd, sp_scr[p], preferred_element_type=F32)
            ca = c_scr[2 * p:2 * p + 1, :] * keep2
            cb = c_scr[2 * p + 1:2 * p + 2, :] * keep2
            cpair = jnp.concatenate([jnp.broadcast_to(ca, (TKS, TQ)),
                                     jnp.broadcast_to(cb, (TKS, TQ))], axis=0)
            a_scr[p] = jnp.exp2(z_scr[p] - suf - cpair).astype(BF16)
            ca = ca + suf[0:1, :]
            cb = cb + suf[TKS:TKS + 1, :]
            c_scr[2 * p:2 * p + 1, :] = ca
            c_scr[2 * p + 1:2 * p + 2, :] = cb
            m = jnp.minimum(ca, cb)
            cmin = m if cmin is None else jnp.minimum(cmin, m)
        dead = jnp.min(cmin) >= SB_DEAD

        qs = pl.multiple_of(tab_ref[SB_Q, j1], TQ)
        ks = pl.multiple_of(tab_ref[SB_K, j1], TKS)
        mask_bias = mask_ref[tab_ref[SB_MSEL, j1]]
        for p in pairs:
            cols = slice(p * LANES, (p + 1) * LANES)
            q = q_ref[0, pl.ds(qs, TQ), cols]
            kbd = jnp.concatenate([klo_ref[0, pl.ds(ks, TKS), cols],
                                   khi_ref[0, pl.ds(ks, TKS), cols]], axis=0)
            z = _nt_dot(kbd, q) + mask_bias
            sp = jnp.maximum(z, 0.0) + jnp.log2(1.0 + jnp.exp2(-jnp.abs(z)))
            z_scr[p] = z
            sp_scr[p] = sp.astype(BF16)

        blk3 = tab_ref[SB_BLK, j3]
        last3 = (i3 < n_items) & ((i2 >= n_items) | (tab_ref[SB_BLK, j2] != blk3))

        @pl.when(last3)
        def _():
            qf = pl.multiple_of(tab_ref[SB_Q, j3], TQ)
            for p in pairs:
                o = acc_scr[p]
                o2 = o * o
                msa = jnp.mean(o2[:HEAD_DIM], axis=0, keepdims=True)
                msb = jnp.mean(o2[HEAD_DIM:], axis=0, keepdims=True)
                ms = jnp.concatenate([jnp.broadcast_to(msa, (HEAD_DIM, TQ)),
                                      jnp.broadcast_to(msb, (HEAD_DIM, TQ))], axis=0)
                o = o * lax.rsqrt(ms + EPS) * g_ref[...]
                o_ref[0, pl.ds(qf, TQ), p * LANES:(p + 1) * LANES] = o.T.astype(BF16)
                acc_scr[p] = jnp.zeros((LANES, TQ), F32)

        same_blk = (i2 < n_items) & (tab_ref[SB_BLK, j2] == tab_ref[SB_BLK, j1])
        nxt = jnp.where(dead & same_blk, tab_ref[SB_NEXT, j1], i1 + 1)
        nxt = jnp.where(i1 < n_items, jnp.minimum(nxt, n_items), n_items)
        return nxt, i1, i2

    def busy(carry):
        i1, i2, i3 = carry
        return (i1 < n_items) | (i2 < n_items) | (i3 < n_items)

    none = jnp.int32(n_items)
    lax.while_loop(busy, step, (jnp.int32(0), none, none))


def _stick_breaking(q3, klo3, khi3, vs, g_pair_col):
    B, S, _ = q3.shape
    tab = _sb_schedule(S)
    nd = TQ // TKS
    c = (np.arange(2 * TKS) % TKS)[:, None]
    r = np.arange(TQ)[None, :]
    mask = np.stack([np.where(c + d * TKS < r, 0.0, NEG_INF) for d in reversed(range(nd))]
                    + [np.zeros((2 * TKS, TQ))]).astype(np.float32)
    seq_blk = pl.BlockSpec((1, S, SB_WIDTH), lambda b, tab: (b, 0, 1))
    tile = (N_SB_PAIRS, 2 * TKS, TQ)
    grid_spec = pltpu.PrefetchScalarGridSpec(
        num_scalar_prefetch=1,
        grid=(B,),
        in_specs=[
            pl.BlockSpec((LANES, 1), lambda b, tab: (0, 0)),
            pl.BlockSpec((nd + 1, 2 * TKS, TQ), lambda b, tab: (0, 0, 0)),
            seq_blk, seq_blk, seq_blk,
            pl.BlockSpec((2 * SB_WIDTH, S), lambda b, tab: (0, b)),
        ],
        out_specs=pl.BlockSpec((1, S, SB_WIDTH), lambda b, tab: (b, 0, 0)),
        scratch_shapes=[
            pltpu.VMEM(tile, F32),
            pltpu.VMEM(tile, BF16),
            pltpu.VMEM(tile, BF16),
            pltpu.VMEM((N_SB_PAIRS, LANES, TQ), F32),
            pltpu.VMEM((2 * N_SB_PAIRS, TQ), F32),
        ],
    )
    return pl.pallas_call(
        _stick_break_kernel,
        grid_spec=grid_spec,
        out_shape=jax.ShapeDtypeStruct((B, S, SB_WIDTH), BF16),
        compiler_params=pltpu.CompilerParams(
            dimension_semantics=("arbitrary",), vmem_limit_bytes=VMEM_LIMIT),
        name="stick_break",
    )(jnp.asarray(tab), g_pair_col, jnp.asarray(mask), q3, klo3, khi3, vs)


def _out_ffn_kernel(x_ref, od_ref, os_ref, wo_ref, g2_ref, wu_ref, wd_ref, gf_ref, o_ref):
    h = (x_ref[...]
         + jnp.dot(od_ref[...], wo_ref[:DIFF_WIDTH, :], preferred_element_type=F32)
         + jnp.dot(os_ref[...], wo_ref[DIFF_WIDTH:, :], preferred_element_type=F32))
    ms = jnp.mean(h * h, axis=-1, keepdims=True)
    n2 = (h * lax.rsqrt(ms + EPS) * g2_ref[...]).astype(BF16)
    u = jnp.dot(n2, wu_ref[...], preferred_element_type=F32)
    u = jnp.square(jnp.maximum(u, 0.0)).astype(BF16)
    h = h + jnp.dot(u, wd_ref[...], preferred_element_type=F32)
    ms = jnp.mean(h * h, axis=-1, keepdims=True)
    o_ref[...] = h * lax.rsqrt(ms + EPS) * gf_ref[...]


def _out_ffn(x2, od, osb, wo, g2, wu, wd, gf):
    rows = x2.shape[0]
    const = lambda i: (0, 0)
    return pl.pallas_call(
        _out_ffn_kernel,
        grid=(rows // TM,),
        in_specs=[
            pl.BlockSpec((TM, D_MODEL), lambda i: (i, 0)),
            pl.BlockSpec((TM, DIFF_WIDTH), lambda i: (i, 0)),
            pl.BlockSpec((TM, SB_WIDTH), lambda i: (i, 0)),
            pl.BlockSpec((D_MODEL, D_MODEL), const),
            pl.BlockSpec((1, D_MODEL), const),
            pl.BlockSpec((D_MODEL, D_FF), const),
            pl.BlockSpec((D_FF, D_MODEL), const),
            pl.BlockSpec((1, D_MODEL), const),
        ],
        out_specs=pl.BlockSpec((TM, D_MODEL), lambda i: (i, 0)),
        out_shape=jax.ShapeDtypeStruct((rows, D_MODEL), F32),
        compiler_params=pltpu.CompilerParams(
            dimension_semantics=("arbitrary",), vmem_limit_bytes=VMEM_LIMIT),
        name="out_ffn",
    )(x2, od, osb, wo, g2, wu, wd, gf)


def kernel(x, norm1_g, w_in, lambda_q1, lambda_k1, lambda_q2, lambda_k2, diff_subln_g,
           sb_norm_g, w_out, norm2_g, w_up, w_down, final_norm_g):
    B, S, D = x.shape
    x2 = x.reshape(B * S, D)
    slopes2 = (np.exp2(-8.0 * np.arange(1, N_DIFF_HEADS + 1) / N_DIFF_HEADS) * LOG2E).astype(np.float32)

    w = w_in[0]
    d1, d2, d3 = DIFF_WIDTH, 2 * DIFF_WIDTH, 3 * DIFF_WIDTH
    wq = jnp.concatenate([w[:, :d1], w[:, d3:d3 + SB_WIDTH]], axis=1).astype(BF16)
    wk = jnp.concatenate([w[:, d1:d2], w[:, d3 + SB_WIDTH:d3 + 2 * SB_WIDTH]], axis=1).astype(BF16)
    wvt = jnp.concatenate([w[:, d2:d3], w[:, d3 + 2 * SB_WIDTH:]], axis=1).T.astype(BF16)

    q, klo, khi, vd, vs = _norm_proj(x2, norm1_g[0][None, :], wq, wk, wvt)
    q3, klo3, khi3 = (a.reshape(B, S, Q_WIDTH) for a in (q, klo, khi))

    o_diff = _diff_attention(q3, klo3, khi3, vd, slopes2, lambda_q1, lambda_k1, lambda_q2,
                             lambda_k2, diff_subln_g[0][:, None])
    g_pair = jnp.concatenate([sb_norm_g[0], sb_norm_g[0]])[:, None]
    o_sb = _stick_breaking(q3, klo3, khi3, vs, g_pair)

    out = _out_ffn(x2, o_diff.reshape(B * S, DIFF_WIDTH), o_sb.reshape(B * S, SB_WIDTH),
                   w_out[0].astype(BF16), norm2_g[0][None, :], w_up[0].astype(BF16),
                   w_down[0].astype(BF16), final_norm_g[None, :])
    return out.reshape(B, S, D)
```

```python
import math

import numpy as np
import jax
import jax.numpy as jnp
from jax import lax
from jax.experimental import pallas as pl
from jax.experimental.pallas import tpu as pltpu

D_MODEL = 1024
HEAD_DIM = 64
CHUNK = 64
N_DIFF_HEADS = 4
N_SB_HEADS = 8
N_SB_PAIRS = N_SB_HEADS // 2
DIFF_V_DIM = 2 * HEAD_DIM
DIFF_WIDTH = N_DIFF_HEADS * DIFF_V_DIM
SB_WIDTH = N_SB_HEADS * HEAD_DIM
Q_WIDTH = DIFF_WIDTH + SB_WIDTH
D_FF = 4 * D_MODEL
EPS = 1e-6
NEG_INF = -1e30
LAMBDA_INIT = 0.8 - 0.6 * math.exp(-0.3 * 0)
LOG2E = 1.4426950408889634

LANES = 128
BF16_ROWS = 16
TQ = 256
TKD = 256
TKS = 128
TM = 512
VD_ROWS = DIFF_V_DIM + BF16_ROWS
VMEM_LIMIT = 56 * 1024 * 1024

F32 = jnp.float32
BF16 = jnp.bfloat16


def _nt_dot(a, b):
    return lax.dot_general(a, b, (((1,), (1,)), ((), ())), preferred_element_type=F32)


def _norm_proj_kernel(x_ref, g_ref, wq_ref, wk_ref, wvt_ref, oq_ref, oklo_ref, okhi_ref,
                      ovd_ref, ovs_ref):
    x = x_ref[...]
    ms = jnp.mean(x * x, axis=-1, keepdims=True)
    n = (x * lax.rsqrt(ms + EPS) * g_ref[...]).astype(BF16)
    chunk = 512
    lane = lax.broadcasted_iota(jnp.int32, (1, chunk), 1) % LANES
    lo = jnp.where(lane < HEAD_DIM, 1.0, 0.0)
    hi = 1.0 - lo
    qscale = LOG2E / math.sqrt(HEAD_DIM)
    for c in range(Q_WIDTH // chunk):
        sl = slice(c * chunk, (c + 1) * chunk)
        oq_ref[:, sl] = (jnp.dot(n, wq_ref[:, sl], preferred_element_type=F32) * qscale).astype(BF16)
        kk = jnp.dot(n, wk_ref[:, sl], preferred_element_type=F32)
        oklo_ref[:, sl] = (kk * lo).astype(BF16)
        okhi_ref[:, sl] = (kk * hi).astype(BF16)
    vd = _nt_dot(wvt_ref[:DIFF_WIDTH, :], n)
    prow = lax.broadcasted_iota(jnp.int32, (BF16_ROWS, TM), 0)
    ones_rows = jnp.where(prow == 0, 1.0, 0.0).astype(BF16)
    for h in range(N_DIFF_HEADS):
        ovd_ref[h * VD_ROWS:h * VD_ROWS + DIFF_V_DIM, :] = (
            vd[h * DIFF_V_DIM:(h + 1) * DIFF_V_DIM].astype(BF16))
        ovd_ref[h * VD_ROWS + DIFF_V_DIM:(h + 1) * VD_ROWS, :] = ones_rows
    vs = _nt_dot(wvt_ref[DIFF_WIDTH:, :], n)
    row = lax.broadcasted_iota(jnp.int32, (SB_WIDTH, 1), 0) % LANES
    ra = jnp.where(row < HEAD_DIM, 1.0, 0.0)
    ovs_ref[:SB_WIDTH, :] = (vs * ra).astype(BF16)
    ovs_ref[SB_WIDTH:, :] = (vs * (1.0 - ra)).astype(BF16)


def _norm_proj(x2, g, wq, wk, wvt):
    rows = x2.shape[0]
    const = lambda i: (0, 0)
    row_blk = lambda w: pl.BlockSpec((TM, w), lambda i: (i, 0))
    col_blk = lambda r: pl.BlockSpec((r, TM), lambda i: (0, i))
    return pl.pallas_call(
        _norm_proj_kernel,
        grid=(rows // TM,),
        in_specs=[
            row_blk(D_MODEL),
            pl.BlockSpec((1, D_MODEL), const),
            pl.BlockSpec((D_MODEL, Q_WIDTH), const),
            pl.BlockSpec((D_MODEL, Q_WIDTH), const),
            pl.BlockSpec((Q_WIDTH, D_MODEL), const),
        ],
        out_specs=[row_blk(Q_WIDTH), row_blk(Q_WIDTH), row_blk(Q_WIDTH),
                   col_blk(N_DIFF_HEADS * VD_ROWS), col_blk(2 * SB_WIDTH)],
        out_shape=[jax.ShapeDtypeStruct((rows, Q_WIDTH), BF16),
                   jax.ShapeDtypeStruct((rows, Q_WIDTH), BF16),
                   jax.ShapeDtypeStruct((rows, Q_WIDTH), BF16),
                   jax.ShapeDtypeStruct((N_DIFF_HEADS * VD_ROWS, rows), BF16),
                   jax.ShapeDtypeStruct((2 * SB_WIDTH, rows), BF16)],
        compiler_params=pltpu.CompilerParams(
            dimension_semantics=("arbitrary",), vmem_limit_bytes=VMEM_LIMIT),
        name="norm_proj",
    )(x2, g, wq, wk, wvt)


def _stage_table(items, lags, fields, defaults):
    n = len(items)
    nsteps = n + max(lags)
    nsteps += nsteps % 2
    rows = [(s, f) for s in range(len(lags)) for f in fields[s]]
    tab = np.zeros((len(rows), nsteps), np.int32)
    for ri, (s, f) in enumerate(rows):
        for t in range(nsteps):
            idx = t - lags[s]
            if 0 <= idx < n:
                tab[ri, t] = items[idx][f]
            elif f in defaults:
                tab[ri, t] = defaults[f]
            else:
                tab[ri, t] = items[min(max(idx, 0), n - 1)][f]
    return tab, {(s, f): ri for ri, (s, f) in enumerate(rows)}


def _diff_schedule(seq):
    items = []
    for i in range(seq // TQ):
        t0 = i * TQ
        items.append(dict(q=t0, k=t0, kind=0, keep=0, dt=0, last=0))
        for j in range(i):
            items.append(dict(q=t0, k=j * TKD, kind=1, keep=1, dt=t0 - j * TKD, last=0))
        items[-1]["last"] = 1
    return _stage_table(items, lags=(0, 1, 2),
                        fields=(("q", "k"), ("kind", "keep", "dt"), ("k", "q", "last")),
                        defaults=dict(keep=1, last=0))


def _sb_schedule(seq):
    items = []
    nd = TQ // TKS
    for i in range(seq // TQ):
        t0 = i * TQ
        first = len(items)
        for d in reversed(range(nd)):
            items.append([t0, t0 + d * TKS, nd - 1 - d, 1, i, 0])
        for j in reversed(range(nd * i)):
            items.append([t0, j * TKS, nd, 1, i, 0])
        items[first][3] = 0
        for it in items[first:]:
            it[5] = len(items)
    return np.asarray(items, np.int32).T.copy()


def _make_diff_kernel(rix):
    heads = range(N_DIFF_HEADS)

    def kernel(tab_ref, slopes_ref, lq1_ref, lk1_ref, lq2_ref, lk2_ref, g_ref, bias_ref,
               q_ref, klo_ref, khi_ref, vd_ref, o_ref,
               s0_scr, s1_scr, p0_scr, p1_scr, al0_scr, al1_scr, m_scr, acc_scr):
        s_scr, p_scr, alpha_scr = (s0_scr, s1_scr), (p0_scr, p1_scr), (al0_scr, al1_scr)
        lam = (jnp.exp(jnp.sum(lq1_ref[...] * lk1_ref[...], axis=1, keepdims=True))
               - jnp.exp(jnp.sum(lq2_ref[...] * lk2_ref[...], axis=1, keepdims=True))
               + LAMBDA_INIT)

        for ref in s_scr + p_scr + alpha_scr:
            ref[...] = jnp.zeros(ref.shape, ref.dtype)
        m_scr[...] = jnp.full(m_scr.shape, NEG_INF, F32)
        acc_scr[...] = jnp.zeros(acc_scr.shape, F32)

        def step(t, cur):
            prev = 1 - cur
            qs = pl.multiple_of(tab_ref[rix[0, "q"], t], TQ)
            ks = pl.multiple_of(tab_ref[rix[0, "k"], t], TKD)
            for h in heads:
                cols = slice(h * LANES, (h + 1) * LANES)
                q = q_ref[0, pl.ds(qs, TQ), cols]
                kbd = jnp.concatenate([klo_ref[0, pl.ds(ks, TKD), cols],
                                       khi_ref[0, pl.ds(ks, TKD), cols]], axis=0)
                s_scr[cur][h] = _nt_dot(kbd, q)

            ks = pl.multiple_of(tab_ref[rix[2, "k"], t], TKD)
            for h in heads:
                vt = vd_ref[h * VD_ROWS:(h + 1) * VD_ROWS, pl.ds(ks, TKD)]
                for m in range(2):
                    row = 2 * h + m
                    pv = jnp.dot(vt, p_scr[prev][row], preferred_element_type=F32)
                    acc_scr[row] = acc_scr[row] * alpha_scr[prev][row:row + 1, :] + pv

            kind = tab_ref[rix[1, "kind"], t]
            keep = tab_ref[rix[1, "keep"], t].astype(F32)
            dt = tab_ref[rix[1, "dt"], t].astype(F32)
            for h in heads:
                shift = -slopes_ref[h] * dt
                for m in range(2):
                    row = 2 * h + m
                    for half in range(TQ // LANES):
                        ls = slice(half * LANES, (half + 1) * LANES)
                        u = s_scr[prev][h, m * TKD:(m + 1) * TKD, ls] + bias_ref[h, kind, :, ls]
                        m_old = m_scr[row:row + 1, ls] * keep + NEG_INF * (1.0 - keep)
                        m_new = jnp.maximum(m_old, jnp.max(u, axis=0, keepdims=True) + shift)
                        p_scr[cur][row, :, ls] = jnp.exp2(u - (m_new - shift)).astype(BF16)
                        alpha_scr[cur][row:row + 1, ls] = jnp.exp2(m_old - m_new)
                        m_scr[row:row + 1, ls] = m_new

            @pl.when(tab_ref[rix[2, "last"], t] == 1)
            def _():
                qf = pl.multiple_of(tab_ref[rix[2, "q"], t], TQ)
                for h in heads:
                    a1 = acc_scr[2 * h]
                    a2 = acc_scr[2 * h + 1]
                    o = (a1[:DIFF_V_DIM] / a1[DIFF_V_DIM:DIFF_V_DIM + 1]
                         - lam * (a2[:DIFF_V_DIM] / a2[DIFF_V_DIM:DIFF_V_DIM + 1]))
                    ms = jnp.mean(o * o, axis=0, keepdims=True)
                    o = o * lax.rsqrt(ms + EPS) * g_ref[...] * (1.0 - LAMBDA_INIT)
                    o_ref[0, pl.ds(qf, TQ), h * LANES:(h + 1) * LANES] = o.T.astype(BF16)

        def two_steps(i, carry):
            step(2 * i, 0)
            step(2 * i + 1, 1)
            return carry

        lax.fori_loop(0, tab_ref.shape[1] // 2, two_steps, 0)

    return kernel


def _diff_attention(q3, klo3, khi3, vd, slopes2, lq1, lk1, lq2, lk2, subln_g_col):
    B, S, _ = q3.shape
    tab, rix = _diff_schedule(S)
    c = np.arange(TKD)[:, None]
    r = np.arange(TQ)[None, :]
    slopes = np.asarray(slopes2, np.float64)[:, None, None]
    diag = np.where((c // CHUNK) <= (r // CHUNK), -slopes * np.abs(r - c), NEG_INF)
    full = -slopes * (r - c)
    bias = np.stack([diag, full], axis=1).astype(np.float32)
    vec = pl.BlockSpec((1, HEAD_DIM), lambda b, tab: (0, 0))
    seq_blk = pl.BlockSpec((1, S, DIFF_WIDTH), lambda b, tab: (b, 0, 0))
    nmap = 2 * N_DIFF_HEADS
    grid_spec = pltpu.PrefetchScalarGridSpec(
        num_scalar_prefetch=1,
        grid=(B,),
        in_specs=[
            pl.BlockSpec(memory_space=pltpu.SMEM),
            vec, vec, vec, vec,
            pl.BlockSpec((DIFF_V_DIM, 1), lambda b, tab: (0, 0)),
            pl.BlockSpec((N_DIFF_HEADS, 2, TKD, TQ), lambda b, tab: (0, 0, 0, 0)),
            seq_blk, seq_blk, seq_blk,
            pl.BlockSpec((N_DIFF_HEADS * VD_ROWS, S), lambda b, tab: (0, b)),
        ],
        out_specs=seq_blk,
        scratch_shapes=[
            pltpu.VMEM((N_DIFF_HEADS, 2 * TKD, TQ), F32),
            pltpu.VMEM((N_DIFF_HEADS, 2 * TKD, TQ), F32),
            pltpu.VMEM((nmap, TKD, TQ), BF16),
            pltpu.VMEM((nmap, TKD, TQ), BF16),
            pltpu.VMEM((nmap, TQ), F32),
            pltpu.VMEM((nmap, TQ), F32),
            pltpu.VMEM((nmap, TQ), F32),
            pltpu.VMEM((nmap, VD_ROWS, TQ), F32),
        ],
    )
    return pl.pallas_call(
        _make_diff_kernel(rix),
        grid_spec=grid_spec,
        out_shape=jax.ShapeDtypeStruct((B, S, DIFF_WIDTH), BF16),
        compiler_params=pltpu.CompilerParams(
            dimension_semantics=("arbitrary",), vmem_limit_bytes=VMEM_LIMIT),
        name="diff_attn",
    )(jnp.asarray(tab), slopes2, lq1, lk1, lq2, lk2, subln_g_col, jnp.asarray(bias),
      q3, klo3, khi3, vd)


SB_Q, SB_K, SB_MSEL, SB_KEEP, SB_BLK, SB_NEXT = range(6)
SB_DEAD = 160.0


def _stick_break_kernel(tab_ref, g_ref, mask_ref, q_ref, klo_ref, khi_ref, vs_ref, o_ref,
                        z_scr, sp_scr, a_scr, acc_scr, c_scr):
    n_items = tab_ref.shape[1]
    pairs = range(N_SB_PAIRS)
    su = lax.broadcasted_iota(jnp.int32, (2 * TKS, 2 * TKS), 0)
    ju = lax.broadcasted_iota(jnp.int32, (2 * TKS, 2 * TKS), 1)
    utbd = jnp.where((ju >= su) & ((ju // TKS) == (su // TKS)), 1.0, 0.0).astype(BF16)

    z_scr[...] = jnp.full(z_scr.shape, NEG_INF, F32)
    sp_scr[...] = jnp.zeros(sp_scr.shape, BF16)
    a_scr[...] = jnp.zeros(a_scr.shape, BF16)
    acc_scr[...] = jnp.zeros(acc_scr.shape, F32)
    c_scr[...] = jnp.zeros(c_scr.shape, F32)

    def step(carry):
        i1, i2, i3 = carry
        j1 = jnp.minimum(i1, n_items - 1)
        j2 = jnp.minimum(i2, n_items - 1)
        j3 = jnp.minimum(i3, n_items - 1)

        ks = pl.multiple_of(tab_ref[SB_K, j3], TKS)
        for p in pairs:
            rows = slice(p * LANES, (p + 1) * LANES)
            vbd = jnp.concatenate([vs_ref[rows, pl.ds(ks, TKS)],
                                   vs_ref[SB_WIDTH + p * LANES:SB_WIDTH + (p + 1) * LANES,
                                          pl.ds(ks, TKS)]], axis=1)
            acc_scr[p] = acc_scr[p] + jnp.dot(vbd, a_scr[p], preferred_element_type=F32)

        keep2 = tab_ref[SB_KEEP, j2].astype(F32)
        cmin = None
        for p in pairs:
            suf = jnp.dot(utbd, sp_scr[p], preferred_element_type=F32)
            ca = c_scr[2 * p:2 * p + 1, :] * keep2
            cb = c_scr[2 * p + 1:2 * p + 2, :] * keep2
            cpair = jnp.concatenate([jnp.broadcast_to(ca, (TKS, TQ)),
                                     jnp.broadcast_to(cb, (TKS, TQ))], axis=0)
            a_scr[p] = jnp.exp2(z_scr[p] - suf - cpair).astype(BF16)
            ca = ca + suf[0:1, :]
            cb = cb + suf[TKS:TKS + 1, :]
            c_scr[2 * p:2 * p + 1, :] = ca
            c_scr[2 * p + 1:2 * p + 2, :] = cb
            m = jnp.minimum(ca, cb)
            cmin = m if cmin is None else jnp.minimum(cmin, m)
        dead = jnp.min(cmin) >= SB_DEAD

        qs = pl.multiple_of(tab_ref[SB_Q, j1], TQ)
        ks = pl.multiple_of(tab_ref[SB_K, j1], TKS)
        mask_bias = mask_ref[tab_ref[SB_MSEL, j1]]
        for p in pairs:
            cols = slice(p * LANES, (p + 1) * LANES)
            q = q_ref[0, pl.ds(qs, TQ), cols]
            kbd = jnp.concatenate([klo_ref[0, pl.ds(ks, TKS), cols],
                                   khi_ref[0, pl.ds(ks, TKS), cols]], axis=0)
            z = _nt_dot(kbd, q) + mask_bias
            sp = jnp.maximum(z, 0.0) + jnp.log2(1.0 + jnp.exp2(-jnp.abs(z)))
            z_scr[p] = z
            sp_scr[p] = sp.astype(BF16)

        blk3 = tab_ref[SB_BLK, j3]
        last3 = (i3 < n_items) & ((i2 >= n_items) | (tab_ref[SB_BLK, j2] != blk3))

        @pl.when(last3)
        def _():
            qf = pl.multiple_of(tab_ref[SB_Q, j3], TQ)
            for p in pairs:
                o = acc_scr[p]
                o2 = o * o
                msa = jnp.mean(o2[:HEAD_DIM], axis=0, keepdims=True)
                msb = jnp.mean(o2[HEAD_DIM:], axis=0, keepdims=True)
                ms = jnp.concatenate([jnp.broadcast_to(msa, (HEAD_DIM, TQ)),
                                      jnp.broadcast_to(msb, (HEAD_DIM, TQ))], axis=0)
                o = o * lax.rsqrt(ms + EPS) * g_ref[...]
                o_ref[0, pl.ds(qf, TQ), p * LANES:(p + 1) * LANES] = o.T.astype(BF16)
                acc_scr[p] = jnp.zeros((LANES, TQ), F32)

        same_blk = (i2 < n_items) & (tab_ref[SB_BLK, j2] == tab_ref[SB_BLK, j1])
        nxt = jnp.where(dead & same_blk, tab_ref[SB_NEXT, j1], i1 + 1)
        nxt = jnp.where(i1 < n_items, jnp.minimum(nxt, n_items), n_items)
        return nxt, i1, i2

    def busy(carry):
        i1, i2, i3 = carry
        return (i1 < n_items) | (i2 < n_items) | (i3 < n_items)

    none = jnp.int32(n_items)
    lax.while_loop(busy, step, (jnp.int32(0), none, none))


def _stick_breaking(q3, klo3, khi3, vs, g_pair_col):
    B, S, _ = q3.shape
    tab = _sb_schedule(S)
    nd = TQ // TKS
    c = (np.arange(2 * TKS) % TKS)[:, None]
    r = np.arange(TQ)[None, :]
    mask = np.stack([np.where(c + d * TKS < r, 0.0, NEG_INF) for d in reversed(range(nd))]
                    + [np.zeros((2 * TKS, TQ))]).astype(np.float32)
    seq_blk = pl.BlockSpec((1, S, SB_WIDTH), lambda b, tab: (b, 0, 1))
    tile = (N_SB_PAIRS, 2 * TKS, TQ)
    grid_spec = pltpu.PrefetchScalarGridSpec(
        num_scalar_prefetch=1,
        grid=(B,),
        in_specs=[
            pl.BlockSpec((LANES, 1), lambda b, tab: (0, 0)),
            pl.BlockSpec((nd + 1, 2 * TKS, TQ), lambda b, tab: (0, 0, 0)),
            seq_blk, seq_blk, seq_blk,
            pl.BlockSpec((2 * SB_WIDTH, S), lambda b, tab: (0, b)),
        ],
        out_specs=pl.BlockSpec((1, S, SB_WIDTH), lambda b, tab: (b, 0, 0)),
        scratch_shapes=[
            pltpu.VMEM(tile, F32),
            pltpu.VMEM(tile, BF16),
            pltpu.VMEM(tile, BF16),
            pltpu.VMEM((N_SB_PAIRS, LANES, TQ), F32),
            pltpu.VMEM((2 * N_SB_PAIRS, TQ), F32),
        ],
    )
    return pl.pallas_call(
        _stick_break_kernel,
        grid_spec=grid_spec,
        out_shape=jax.ShapeDtypeStruct((B, S, SB_WIDTH), BF16),
        compiler_params=pltpu.CompilerParams(
            dimension_semantics=("arbitrary",), vmem_limit_bytes=VMEM_LIMIT),
        name="stick_break",
    )(jnp.asarray(tab), g_pair_col, jnp.asarray(mask), q3, klo3, khi3, vs)


def _out_ffn_kernel(x_ref, od_ref, os_ref, wo_ref, g2_ref, wu_ref, wd_ref, gf_ref, o_ref):
    h = (x_ref[...]
         + jnp.dot(od_ref[...], wo_ref[:DIFF_WIDTH, :], preferred_element_type=F32)
         + jnp.dot(os_ref[...], wo_ref[DIFF_WIDTH:, :], preferred_element_type=F32))
    ms = jnp.mean(h * h, axis=-1, keepdims=True)
    n2 = (h * lax.rsqrt(ms + EPS) * g2_ref[...]).astype(BF16)
    u = jnp.dot(n2, wu_ref[...], preferred_element_type=F32)
    u = jnp.square(jnp.maximum(u, 0.0)).astype(BF16)
    h = h + jnp.dot(u, wd_ref[...], preferred_element_type=F32)
    ms = jnp.mean(h * h, axis=-1, keepdims=True)
    o_ref[...] = h * lax.rsqrt(ms + EPS) * gf_ref[...]


def _out_ffn(x2, od, osb, wo, g2, wu, wd, gf):
    rows = x2.shape[0]
    const = lambda i: (0, 0)
    return pl.pallas_call(
        _out_ffn_kernel,
        grid=(rows // TM,),
        in_specs=[
            pl.BlockSpec((TM, D_MODEL), lambda i: (i, 0)),
            pl.BlockSpec((TM, DIFF_WIDTH), lambda i: (i, 0)),
            pl.BlockSpec((TM, SB_WIDTH), lambda i: (i, 0)),
            pl.BlockSpec((D_MODEL, D_MODEL), const),
            pl.BlockSpec((1, D_MODEL), const),
            pl.BlockSpec((D_MODEL, D_FF), const),
            pl.BlockSpec((D_FF, D_MODEL), const),
            pl.BlockSpec((1, D_MODEL), const),
        ],
        out_specs=pl.BlockSpec((TM, D_MODEL), lambda i: (i, 0)),
        out_shape=jax.ShapeDtypeStruct((rows, D_MODEL), F32),
        compiler_params=pltpu.CompilerParams(
            dimension_semantics=("arbitrary",), vmem_limit_bytes=VMEM_LIMIT),
        name="out_ffn",
    )(x2, od, osb, wo, g2, wu, wd, gf)


def kernel(x, norm1_g, w_in, lambda_q1, lambda_k1, lambda_q2, lambda_k2, diff_subln_g,
           sb_norm_g, w_out, norm2_g, w_up, w_down, final_norm_g):
    B, S, D = x.shape
    x2 = x.reshape(B * S, D)
    slopes2 = (np.exp2(-8.0 * np.arange(1, N_DIFF_HEADS + 1) / N_DIFF_HEADS) * LOG2E).astype(np.float32)

    w = w_in[0]
    d1, d2, d3 = DIFF_WIDTH, 2 * DIFF_WIDTH, 3 * DIFF_WIDTH
    wq = jnp.concatenate([w[:, :d1], w[:, d3:d3 + SB_WIDTH]], axis=1).astype(BF16)
    wk = jnp.concatenate([w[:, d1:d2], w[:, d3 + SB_WIDTH:d3 + 2 * SB_WIDTH]], axis=1).astype(BF16)
    wvt = jnp.concatenate([w[:, d2:d3], w[:, d3 + 2 * SB_WIDTH:]], axis=1).T.astype(BF16)

    q, klo, khi, vd, vs = _norm_proj(x2, norm1_g[0][None, :], wq, wk, wvt)
    q3, klo3, khi3 = (a.reshape(B, S, Q_WIDTH) for a in (q, klo, khi))

    o_diff = _diff_attention(q3, klo3, khi3, vd, slopes2, lambda_q1, lambda_k1, lambda_q2,
                             lambda_k2, diff_subln_g[0][:, None])
    g_pair = jnp.concatenate([sb_norm_g[0], sb_norm_g[0]])[:, None]
    o_sb = _stick_breaking(q3, klo3, khi3, vs, g_pair)

    out = _out_ffn(x2, o_diff.reshape(B * S, DIFF_WIDTH), o_sb.reshape(B * S, SB_WIDTH),
                   w_out[0].astype(BF16), norm2_g[0][None, :], w_up[0].astype(BF16),
                   w_down[0].astype(BF16), final_norm_g[None, :])
    return out.reshape(B, S, D)
```

```python
import math

import numpy as np
import jax
import jax.numpy as jnp
from jax import lax
from jax.experimental import pallas as pl
from jax.experimental.pallas import tpu as pltpu

D_MODEL = 1024
HEAD_DIM = 64
CHUNK = 64
N_DIFF_HEADS = 4
N_SB_HEADS = 8
N_SB_PAIRS = N_SB_HEADS // 2
DIFF_V_DIM = 2 * HEAD_DIM
DIFF_WIDTH = N_DIFF_HEADS * DIFF_V_DIM
SB_WIDTH = N_SB_HEADS * HEAD_DIM
Q_WIDTH = DIFF_WIDTH + SB_WIDTH
D_FF = 4 * D_MODEL
EPS = 1e-6
NEG_INF = -1e30
LAMBDA_INIT = 0.8 - 0.6 * math.exp(-0.3 * 0)
LOG2E = 1.4426950408889634

LANES = 128
BF16_ROWS = 16
TQ = 256
TKD = 256
TKS = 128
TM = 512
VD_ROWS = DIFF_V_DIM + BF16_ROWS
VMEM_LIMIT = 56 * 1024 * 1024

F32 = jnp.float32
BF16 = jnp.bfloat16


def _nt_dot(a, b):
    return lax.dot_general(a, b, (((1,), (1,)), ((), ())), preferred_element_type=F32)


def _norm_proj_kernel(x_ref, g_ref, win_ref, wu_ref, wd_ref, wo_ref,
                      oq_ref, oklo_ref, okhi_ref, ovd_ref, ovs_ref, owu_ref, owd_ref, owo_ref,
                      wq_ref, wk_ref, wvt_ref):
    @pl.when(pl.program_id(0) == 0)
    def _():
        d1, d2, d3 = DIFF_WIDTH, 2 * DIFF_WIDTH, 3 * DIFF_WIDTH
        s1, s2 = d3 + SB_WIDTH, d3 + 2 * SB_WIDTH
        wq_ref[:, :DIFF_WIDTH] = win_ref[:, :d1].astype(BF16)
        wq_ref[:, DIFF_WIDTH:] = win_ref[:, d3:s1].astype(BF16)
        wk_ref[:, :DIFF_WIDTH] = win_ref[:, d1:d2].astype(BF16)
        wk_ref[:, DIFF_WIDTH:] = win_ref[:, s1:s2].astype(BF16)
        wvt_ref[:DIFF_WIDTH, :] = win_ref[:, d2:d3].T.astype(BF16)
        wvt_ref[DIFF_WIDTH:, :] = win_ref[:, s2:].T.astype(BF16)

    owu_ref[...] = wu_ref[...].astype(BF16)
    owd_ref[...] = wd_ref[...].astype(BF16)
    owo_ref[...] = wo_ref[...].astype(BF16)

    x = x_ref[...]
    ms = jnp.mean(x * x, axis=-1, keepdims=True)
    n = (x * lax.rsqrt(ms + EPS) * g_ref[...]).astype(BF16)
    chunk = 512
    lane = lax.broadcasted_iota(jnp.int32, (1, chunk), 1) % LANES
    lo = jnp.where(lane < HEAD_DIM, 1.0, 0.0)
    hi = 1.0 - lo
    qscale = LOG2E / math.sqrt(HEAD_DIM)
    for c in range(Q_WIDTH // chunk):
        sl = slice(c * chunk, (c + 1) * chunk)
        oq_ref[:, sl] = (jnp.dot(n, wq_ref[:, sl], preferred_element_type=F32) * qscale).astype(BF16)
        kk = jnp.dot(n, wk_ref[:, sl], preferred_element_type=F32)
        oklo_ref[:, sl] = (kk * lo).astype(BF16)
        okhi_ref[:, sl] = (kk * hi).astype(BF16)
    vd = _nt_dot(wvt_ref[:DIFF_WIDTH, :], n)
    prow = lax.broadcasted_iota(jnp.int32, (BF16_ROWS, TM), 0)
    ones_rows = jnp.where(prow == 0, 1.0, 0.0).astype(BF16)
    for h in range(N_DIFF_HEADS):
        ovd_ref[h * VD_ROWS:h * VD_ROWS + DIFF_V_DIM, :] = (
            vd[h * DIFF_V_DIM:(h + 1) * DIFF_V_DIM].astype(BF16))
        ovd_ref[h * VD_ROWS + DIFF_V_DIM:(h + 1) * VD_ROWS, :] = ones_rows
    vs = _nt_dot(wvt_ref[DIFF_WIDTH:, :], n)
    row = lax.broadcasted_iota(jnp.int32, (SB_WIDTH, 1), 0) % LANES
    ra = jnp.where(row < HEAD_DIM, 1.0, 0.0)
    ovs_ref[:SB_WIDTH, :] = (vs * ra).astype(BF16)
    ovs_ref[SB_WIDTH:, :] = (vs * (1.0 - ra)).astype(BF16)


def _norm_proj(x2, g, w_in, w_up, w_down, w_out):
    rows = x2.shape[0]
    steps = rows // TM
    const = lambda i: (0, 0)
    row_blk = lambda w: pl.BlockSpec((TM, w), lambda i: (i, 0))
    col_blk = lambda r: pl.BlockSpec((r, TM), lambda i: (0, i))
    wu_blk = pl.BlockSpec((D_MODEL, D_FF // steps), lambda i: (0, i))
    wd_blk = pl.BlockSpec((D_FF // steps, D_MODEL), lambda i: (i, 0))
    wo_blk = pl.BlockSpec((D_MODEL // steps, D_MODEL), lambda i: (i, 0))
    return pl.pallas_call(
        _norm_proj_kernel,
        grid=(steps,),
        in_specs=[
            row_blk(D_MODEL),
            pl.BlockSpec((1, D_MODEL), const),
            pl.BlockSpec(w_in.shape, const),
            wu_blk, wd_blk, wo_blk,
        ],
        out_specs=[row_blk(Q_WIDTH), row_blk(Q_WIDTH), row_blk(Q_WIDTH),
                   col_blk(N_DIFF_HEADS * VD_ROWS), col_blk(2 * SB_WIDTH),
                   wu_blk, wd_blk, wo_blk],
        out_shape=[jax.ShapeDtypeStruct((rows, Q_WIDTH), BF16),
                   jax.ShapeDtypeStruct((rows, Q_WIDTH), BF16),
                   jax.ShapeDtypeStruct((rows, Q_WIDTH), BF16),
                   jax.ShapeDtypeStruct((N_DIFF_HEADS * VD_ROWS, rows), BF16),
                   jax.ShapeDtypeStruct((2 * SB_WIDTH, rows), BF16),
                   jax.ShapeDtypeStruct(w_up.shape, BF16),
                   jax.ShapeDtypeStruct(w_down.shape, BF16),
                   jax.ShapeDtypeStruct(w_out.shape, BF16)],
        scratch_shapes=[pltpu.VMEM((D_MODEL, Q_WIDTH), BF16),
                        pltpu.VMEM((D_MODEL, Q_WIDTH), BF16),
                        pltpu.VMEM((Q_WIDTH, D_MODEL), BF16)],
        compiler_params=pltpu.CompilerParams(
            dimension_semantics=("arbitrary",), vmem_limit_bytes=VMEM_LIMIT),
        name="norm_proj",
    )(x2, g, w_in, w_up, w_down, w_out)


def _stage_table(items, lags, fields, defaults):
    n = len(items)
    nsteps = n + max(lags)
    nsteps += nsteps % 2
    rows = [(s, f) for s in range(len(lags)) for f in fields[s]]
    tab = np.zeros((len(rows), nsteps), np.int32)
    for ri, (s, f) in enumerate(rows):
        for t in range(nsteps):
            idx = t - lags[s]
            if 0 <= idx < n:
                tab[ri, t] = items[idx][f]
            elif f in defaults:
                tab[ri, t] = defaults[f]
            else:
                tab[ri, t] = items[min(max(idx, 0), n - 1)][f]
    return tab, {(s, f): ri for ri, (s, f) in enumerate(rows)}


def _diff_schedule(seq):
    items = []
    for i in range(seq // TQ):
        t0 = i * TQ
        items.append(dict(q=t0, k=t0, kind=0, keep=0, dt=0, last=0))
        for j in range(i):
            items.append(dict(q=t0, k=j * TKD, kind=1, keep=1, dt=t0 - j * TKD, last=0))
        items[-1]["last"] = 1
    return _stage_table(items, lags=(0, 1, 2),
                        fields=(("q", "k"), ("kind", "keep", "dt"), ("k", "q", "last")),
                        defaults=dict(keep=1, last=0))


def _sb_schedule(seq):
    items = []
    nd = TQ // TKS
    for i in range(seq // TQ):
        t0 = i * TQ
        first = len(items)
        for d in reversed(range(nd)):
            items.append([t0, t0 + d * TKS, nd - 1 - d, 1, i, 0])
        for j in reversed(range(nd * i)):
            items.append([t0, j * TKS, nd, 1, i, 0])
        items[first][3] = 0
        for it in items[first:]:
            it[5] = len(items)
    return np.asarray(items, np.int32).T.copy()


def _make_diff_kernel(rix):
    heads = range(N_DIFF_HEADS)

    def kernel(tab_ref, slopes_ref, lq1_ref, lk1_ref, lq2_ref, lk2_ref, g_ref, bias_ref,
               q_ref, klo_ref, khi_ref, vd_ref, o_ref,
               s0_scr, s1_scr, p0_scr, p1_scr, al0_scr, al1_scr, m_scr, acc_scr):
        s_scr, p_scr, alpha_scr = (s0_scr, s1_scr), (p0_scr, p1_scr), (al0_scr, al1_scr)
        lam = (jnp.exp(jnp.sum(lq1_ref[...] * lk1_ref[...], axis=1, keepdims=True))
               - jnp.exp(jnp.sum(lq2_ref[...] * lk2_ref[...], axis=1, keepdims=True))
               + LAMBDA_INIT)

        for ref in s_scr + p_scr + alpha_scr:
            ref[...] = jnp.zeros(ref.shape, ref.dtype)
        m_scr[...] = jnp.full(m_scr.shape, NEG_INF, F32)
        acc_scr[...] = jnp.zeros(acc_scr.shape, F32)

        def step(t, cur):
            prev = 1 - cur
            qs = pl.multiple_of(tab_ref[rix[0, "q"], t], TQ)
            ks = pl.multiple_of(tab_ref[rix[0, "k"], t], TKD)
            for h in heads:
                cols = slice(h * LANES, (h + 1) * LANES)
                q = q_ref[0, pl.ds(qs, TQ), cols]
                kbd = jnp.concatenate([klo_ref[0, pl.ds(ks, TKD), cols],
                                       khi_ref[0, pl.ds(ks, TKD), cols]], axis=0)
                s_scr[cur][h] = _nt_dot(kbd, q)

            ks = pl.multiple_of(tab_ref[rix[2, "k"], t], TKD)
            for h in heads:
                vt = vd_ref[h * VD_ROWS:(h + 1) * VD_ROWS, pl.ds(ks, TKD)]
                for m in range(2):
                    row = 2 * h + m
                    pv = jnp.dot(vt, p_scr[prev][row], preferred_element_type=F32)
                    acc_scr[row] = acc_scr[row] * alpha_scr[prev][row:row + 1, :] + pv

            kind = tab_ref[rix[1, "kind"], t]
            keep = tab_ref[rix[1, "keep"], t].astype(F32)
            dt = tab_ref[rix[1, "dt"], t].astype(F32)
            for h in heads:
                shift = -slopes_ref[h] * dt
                for m in range(2):
                    row = 2 * h + m
                    for half in range(TQ // LANES):
                        ls = slice(half * LANES, (half + 1) * LANES)
                        u = s_scr[prev][h, m * TKD:(m + 1) * TKD, ls] + bias_ref[h, kind, :, ls]
                        m_old = m_scr[row:row + 1, ls] * keep + NEG_INF * (1.0 - keep)
                        m_new = jnp.maximum(m_old, jnp.max(u, axis=0, keepdims=True) + shift)
                        p_scr[cur][row, :, ls] = jnp.exp2(u - (m_new - shift)).astype(BF16)
                        alpha_scr[cur][row:row + 1, ls] = jnp.exp2(m_old - m_new)
                        m_scr[row:row + 1, ls] = m_new

            @pl.when(tab_ref[rix[2, "last"], t] == 1)
            def _():
                qf = pl.multiple_of(tab_ref[rix[2, "q"], t], TQ)
                for h in heads:
                    a1 = acc_scr[2 * h]
                    a2 = acc_scr[2 * h + 1]
                    o = (a1[:DIFF_V_DIM] / a1[DIFF_V_DIM:DIFF_V_DIM + 1]
                         - lam * (a2[:DIFF_V_DIM] / a2[DIFF_V_DIM:DIFF_V_DIM + 1]))
                    ms = jnp.mean(o * o, axis=0, keepdims=True)
                    o = o * lax.rsqrt(ms + EPS) * g_ref[...] * (1.0 - LAMBDA_INIT)
                    o_ref[0, pl.ds(qf, TQ), h * LANES:(h + 1) * LANES] = o.T.astype(BF16)

        def two_steps(i, carry):
            step(2 * i, 0)
            step(2 * i + 1, 1)
            return carry

        lax.fori_loop(0, tab_ref.shape[1] // 2, two_steps, 0)

    return kernel


def _diff_attention(q3, klo3, khi3, vd, slopes2, lq1, lk1, lq2, lk2, subln_g_col):
    B, S, _ = q3.shape
    tab, rix = _diff_schedule(S)
    c = np.arange(TKD)[:, None]
    r = np.arange(TQ)[None, :]
    slopes = np.asarray(slopes2, np.float64)[:, None, None]
    diag = np.where((c // CHUNK) <= (r // CHUNK), -slopes * np.abs(r - c), NEG_INF)
    full = -slopes * (r - c)
    bias = np.stack([diag, full], axis=1).astype(np.float32)
    vec = pl.BlockSpec((1, HEAD_DIM), lambda b, tab: (0, 0))
    seq_blk = pl.BlockSpec((1, S, DIFF_WIDTH), lambda b, tab: (b, 0, 0))
    nmap = 2 * N_DIFF_HEADS
    grid_spec = pltpu.PrefetchScalarGridSpec(
        num_scalar_prefetch=1,
        grid=(B,),
        in_specs=[
            pl.BlockSpec(memory_space=pltpu.SMEM),
            vec, vec, vec, vec,
            pl.BlockSpec((DIFF_V_DIM, 1), lambda b, tab: (0, 0)),
            pl.BlockSpec((N_DIFF_HEADS, 2, TKD, TQ), lambda b, tab: (0, 0, 0, 0)),
            seq_blk, seq_blk, seq_blk,
            pl.BlockSpec((N_DIFF_HEADS * VD_ROWS, S), lambda b, tab: (0, b)),
        ],
        out_specs=seq_blk,
        scratch_shapes=[
            pltpu.VMEM((N_DIFF_HEADS, 2 * TKD, TQ), F32),
            pltpu.VMEM((N_DIFF_HEADS, 2 * TKD, TQ), F32),
            pltpu.VMEM((nmap, TKD, TQ), BF16),
            pltpu.VMEM((nmap, TKD, TQ), BF16),
            pltpu.VMEM((nmap, TQ), F32),
            pltpu.VMEM((nmap, TQ), F32),
            pltpu.VMEM((nmap, TQ), F32),
            pltpu.VMEM((nmap, VD_ROWS, TQ), F32),
        ],
    )
    return pl.pallas_call(
        _make_diff_kernel(rix),
        grid_spec=grid_spec,
        out_shape=jax.ShapeDtypeStruct((B, S, DIFF_WIDTH), BF16),
        compiler_params=pltpu.CompilerParams(
            dimension_semantics=("arbitrary",), vmem_limit_bytes=VMEM_LIMIT),
        name="diff_attn",
    )(jnp.asarray(tab), slopes2, lq1, lk1, lq2, lk2, subln_g_col, jnp.asarray(bias),
      q3, klo3, khi3, vd)


SB_Q, SB_K, SB_MSEL, SB_KEEP, SB_BLK, SB_NEXT = range(6)
SB_DEAD = 160.0


def _stick_break_kernel(tab_ref, g_ref, mask_ref, q_ref, klo_ref, khi_ref, vs_ref, o_ref,
                        z_scr, sp_scr, a_scr, acc_scr, c_scr):
    n_items = tab_ref.shape[1]
    pairs = range(N_SB_PAIRS)
    su = lax.broadcasted_iota(jnp.int32, (2 * TKS, 2 * TKS), 0)
    ju = lax.broadcasted_iota(jnp.int32, (2 * TKS, 2 * TKS), 1)
    utbd = jnp.where((ju >= su) & ((ju // TKS) == (su // TKS)), 1.0, 0.0).astype(BF16)

    z_scr[...] = jnp.full(z_scr.shape, NEG_INF, F32)
    sp_scr[...] = jnp.zeros(sp_scr.shape, BF16)
    a_scr[...] = jnp.zeros(a_scr.shape, BF16)
    acc_scr[...] = jnp.zeros(acc_scr.shape, F32)
    c_scr[...] = jnp.zeros(c_scr.shape, F32)

    def step(carry):
        i1, i2, i3 = carry
        j1 = jnp.minimum(i1, n_items - 1)
        j2 = jnp.minimum(i2, n_items - 1)
        j3 = jnp.minimum(i3, n_items - 1)

        ks = pl.multiple_of(tab_ref[SB_K, j3], TKS)
        for p in pairs:
            rows = slice(p * LANES, (p + 1) * LANES)
            vbd = jnp.concatenate([vs_ref[rows, pl.ds(ks, TKS)],
                                   vs_ref[SB_WIDTH + p * LANES:SB_WIDTH + (p + 1) * LANES,
                                          pl.ds(ks, TKS)]], axis=1)
            acc_scr[p] = acc_scr[p] + jnp.dot(vbd, a_scr[p], preferred_element_type=F32)

        keep2 = tab_ref[SB_KEEP, j2].astype(F32)
        cmin = None
        for p in pairs:
            suf = jnp.dot(utbd, sp_scr[p], preferred_element_type=F32)
            ca = c_scr[2 * p:2 * p + 1, :] * keep2
            cb = c_scr[2 * p + 1:2 * p + 2, :] * keep2
            cpair = jnp.concatenate([jnp.broadcast_to(ca, (TKS, TQ)),
                                     jnp.broadcast_to(cb, (TKS, TQ))], axis=0)
            a_scr[p] = jnp.exp2(z_scr[p] - suf - cpair).astype(BF16)
            ca = ca + suf[0:1, :]
            cb = cb + suf[TKS:TKS + 1, :]
            c_scr[2 * p:2 * p + 1, :] = ca
            c_scr[2 * p + 1:2 * p + 2, :] = cb
            m = jnp.minimum(ca, cb)
            cmin = m if cmin is None else jnp.minimum(cmin, m)
        dead = jnp.min(cmin) >= SB_DEAD

        qs = pl.multiple_of(tab_ref[SB_Q, j1], TQ)
        ks = pl.multiple_of(tab_ref[SB_K, j1], TKS)
        mask_bias = mask_ref[tab_ref[SB_MSEL, j1]]
        for p in pairs:
            cols = slice(p * LANES, (p + 1) * LANES)
            q = q_ref[0, pl.ds(qs, TQ), cols]
            kbd = jnp.concatenate([klo_ref[0, pl.ds(ks, TKS), cols],
                                   khi_ref[0, pl.ds(ks, TKS), cols]], axis=0)
            z = _nt_dot(kbd, q) + mask_bias
            zt = jnp.minimum(z, 126.0)
            sp = jnp.log2(1.0 + jnp.exp2(zt)) + (z - zt)
            z_scr[p] = z
            sp_scr[p] = sp.astype(BF16)

        blk3 = tab_ref[SB_BLK, j3]
        last3 = (i3 < n_items) & ((i2 >= n_items) | (tab_ref[SB_BLK, j2] != blk3))

        @pl.when(last3)
        def _():
            qf = pl.multiple_of(tab_ref[SB_Q, j3], TQ)
            for p in pairs:
                o = acc_scr[p]
                o2 = o * o
                msa = jnp.mean(o2[:HEAD_DIM], axis=0, keepdims=True)
                msb = jnp.mean(o2[HEAD_DIM:], axis=0, keepdims=True)
                ms = jnp.concatenate([jnp.broadcast_to(msa, (HEAD_DIM, TQ)),
                                      jnp.broadcast_to(msb, (HEAD_DIM, TQ))], axis=0)
                o = o * lax.rsqrt(ms + EPS) * g_ref[...]
                o_ref[0, pl.ds(qf, TQ), p * LANES:(p + 1) * LANES] = o.T.astype(BF16)
                acc_scr[p] = jnp.zeros((LANES, TQ), F32)

        same_blk = (i2 < n_items) & (tab_ref[SB_BLK, j2] == tab_ref[SB_BLK, j1])
        nxt = jnp.where(dead & same_blk, tab_ref[SB_NEXT, j1], i1 + 1)
        nxt = jnp.where(i1 < n_items, jnp.minimum(nxt, n_items), n_items)
        return nxt, i1, i2

    def busy(carry):
        i1, i2, i3 = carry
        return (i1 < n_items) | (i2 < n_items) | (i3 < n_items)

    none = jnp.int32(n_items)
    lax.while_loop(busy, step, (jnp.int32(0), none, none))


def _stick_breaking(q3, klo3, khi3, vs, g_pair_col):
    B, S, _ = q3.shape
    tab = _sb_schedule(S)
    nd = TQ // TKS
    c = (np.arange(2 * TKS) % TKS)[:, None]
    r = np.arange(TQ)[None, :]
    mask = np.stack([np.where(c + d * TKS < r, 0.0, NEG_INF) for d in reversed(range(nd))]
                    + [np.zeros((2 * TKS, TQ))]).astype(np.float32)
    seq_blk = pl.BlockSpec((1, S, SB_WIDTH), lambda b, tab: (b, 0, 1))
    tile = (N_SB_PAIRS, 2 * TKS, TQ)
    grid_spec = pltpu.PrefetchScalarGridSpec(
        num_scalar_prefetch=1,
        grid=(B,),
        in_specs=[
            pl.BlockSpec((LANES, 1), lambda b, tab: (0, 0)),
            pl.BlockSpec((nd + 1, 2 * TKS, TQ), lambda b, tab: (0, 0, 0)),
            seq_blk, seq_blk, seq_blk,
            pl.BlockSpec((2 * SB_WIDTH, S), lambda b, tab: (0, b)),
        ],
        out_specs=pl.BlockSpec((1, S, SB_WIDTH), lambda b, tab: (b, 0, 0)),
        scratch_shapes=[
            pltpu.VMEM(tile, F32),
            pltpu.VMEM(tile, BF16),
            pltpu.VMEM(tile, BF16),
            pltpu.VMEM((N_SB_PAIRS, LANES, TQ), F32),
            pltpu.VMEM((2 * N_SB_PAIRS, TQ), F32),
        ],
    )
    return pl.pallas_call(
        _stick_break_kernel,
        grid_spec=grid_spec,
        out_shape=jax.ShapeDtypeStruct((B, S, SB_WIDTH), BF16),
        compiler_params=pltpu.CompilerParams(
            dimension_semantics=("arbitrary",), vmem_limit_bytes=VMEM_LIMIT),
        name="stick_break",
    )(jnp.asarray(tab), g_pair_col, jnp.asarray(mask), q3, klo3, khi3, vs)


def _out_ffn_kernel(x_ref, od_ref, os_ref, wo_ref, g2_ref, wu_ref, wd_ref, gf_ref, o_ref):
    h = (x_ref[...]
         + jnp.dot(od_ref[...], wo_ref[:DIFF_WIDTH, :], preferred_element_type=F32)
         + jnp.dot(os_ref[...], wo_ref[DIFF_WIDTH:, :], preferred_element_type=F32))
    ms = jnp.mean(h * h, axis=-1, keepdims=True)
    n2 = (h * lax.rsqrt(ms + EPS) * g2_ref[...]).astype(BF16)
    u = jnp.dot(n2, wu_ref[...], preferred_element_type=F32)
    u = jnp.square(jnp.maximum(u, 0.0)).astype(BF16)
    h = h + jnp.dot(u, wd_ref[...], preferred_element_type=F32)
    ms = jnp.mean(h * h, axis=-1, keepdims=True)
    o_ref[...] = h * lax.rsqrt(ms + EPS) * gf_ref[...]


def _out_ffn(x2, od, osb, wo, g2, wu, wd, gf):
    rows = x2.shape[0]
    const = lambda i: (0, 0)
    return pl.pallas_call(
        _out_ffn_kernel,
        grid=(rows // TM,),
        in_specs=[
            pl.BlockSpec((TM, D_MODEL), lambda i: (i, 0)),
            pl.BlockSpec((TM, DIFF_WIDTH), lambda i: (i, 0)),
            pl.BlockSpec((TM, SB_WIDTH), lambda i: (i, 0)),
            pl.BlockSpec((D_MODEL, D_MODEL), const),
            pl.BlockSpec((1, D_MODEL), const),
            pl.BlockSpec((D_MODEL, D_FF), const),
            pl.BlockSpec((D_FF, D_MODEL), const),
            pl.BlockSpec((1, D_MODEL), const),
        ],
        out_specs=pl.BlockSpec((TM, D_MODEL), lambda i: (i, 0)),
        out_shape=jax.ShapeDtypeStruct((rows, D_MODEL), F32),
        compiler_params=pltpu.CompilerParams(
            dimension_semantics=("arbitrary",), vmem_limit_bytes=VMEM_LIMIT),
        name="out_ffn",
    )(x2, od, osb, wo, g2, wu, wd, gf)


def kernel(x, norm1_g, w_in, lambda_q1, lambda_k1, lambda_q2, lambda_k2, diff_subln_g,
           sb_norm_g, w_out, norm2_g, w_up, w_down, final_norm_g):
    B, S, D = x.shape
    x2 = x.reshape(B * S, D)
    slopes2 = (np.exp2(-8.0 * np.arange(1, N_DIFF_HEADS + 1) / N_DIFF_HEADS) * LOG2E).astype(np.float32)

    q, klo, khi, vd, vs, wu, wd, wo = _norm_proj(x2, norm1_g[0][None, :], w_in[0], w_up[0],
                                                 w_down[0], w_out[0])
    q3, klo3, khi3 = (a.reshape(B, S, Q_WIDTH) for a in (q, klo, khi))

    o_diff = _diff_attention(q3, klo3, khi3, vd, slopes2, lambda_q1, lambda_k1, lambda_q2,
                             lambda_k2, diff_subln_g[0][:, None])
    g_pair = jnp.concatenate([sb_norm_g[0], sb_norm_g[0]])[:, None]
    o_sb = _stick_breaking(q3, klo3, khi3, vs, g_pair)

    out = _out_ffn(x2, o_diff.reshape(B * S, DIFF_WIDTH), o_sb.reshape(B * S, SB_WIDTH),
                   wo, norm2_g[0][None, :], wu, wd, final_norm_g[None, :])
    return out.reshape(B, S, D)
```

```python
import math

import numpy as np
import jax
import jax.numpy as jnp
from jax import lax
from jax.experimental import pallas as pl
from jax.experimental.pallas import tpu as pltpu

D_MODEL = 1024
HEAD_DIM = 64
CHUNK = 64
N_DIFF_HEADS = 4
N_SB_HEADS = 8
N_SB_PAIRS = N_SB_HEADS // 2
DIFF_V_DIM = 2 * HEAD_DIM
DIFF_WIDTH = N_DIFF_HEADS * DIFF_V_DIM
SB_WIDTH = N_SB_HEADS * HEAD_DIM
Q_WIDTH = DIFF_WIDTH + SB_WIDTH
D_FF = 4 * D_MODEL
EPS = 1e-6
NEG_INF = -1e30
LAMBDA_INIT = 0.8 - 0.6 * math.exp(-0.3 * 0)
LOG2E = 1.4426950408889634

LANES = 128
BF16_ROWS = 16
TQ = 256
TKD = 256
TKS = 128
TM = 512
VD_ROWS = DIFF_V_DIM + BF16_ROWS
VMEM_LIMIT = 56 * 1024 * 1024

F32 = jnp.float32
BF16 = jnp.bfloat16


def _nt_dot(a, b):
    return lax.dot_general(a, b, (((1,), (1,)), ((), ())), preferred_element_type=F32)


def _norm_proj_kernel(x_ref, g_ref, win_ref, wu_ref, wd_ref, wo_ref,
                      oq_ref, oklo_ref, okhi_ref, ovd_ref, ovs_ref, owu_ref, owd_ref, owo_ref,
                      wq_ref, wk_ref, wvt_ref):
    @pl.when(pl.program_id(0) == 0)
    def _():
        d1, d2, d3 = DIFF_WIDTH, 2 * DIFF_WIDTH, 3 * DIFF_WIDTH
        s1, s2 = d3 + SB_WIDTH, d3 + 2 * SB_WIDTH
        wq_ref[:, :DIFF_WIDTH] = win_ref[:, :d1].astype(BF16)
        wq_ref[:, DIFF_WIDTH:] = win_ref[:, d3:s1].astype(BF16)
        wk_ref[:, :DIFF_WIDTH] = win_ref[:, d1:d2].astype(BF16)
        wk_ref[:, DIFF_WIDTH:] = win_ref[:, s1:s2].astype(BF16)
        wvt_ref[:DIFF_WIDTH, :] = win_ref[:, d2:d3].T.astype(BF16)
        wvt_ref[DIFF_WIDTH:, :] = win_ref[:, s2:].T.astype(BF16)

    owu_ref[...] = wu_ref[...].astype(BF16)
    owd_ref[...] = wd_ref[...].astype(BF16)
    owo_ref[...] = wo_ref[...].astype(BF16)

    x = x_ref[...]
    ms = jnp.mean(x * x, axis=-1, keepdims=True)
    n = (x * lax.rsqrt(ms + EPS) * g_ref[...]).astype(BF16)
    chunk = 512
    lane = lax.broadcasted_iota(jnp.int32, (1, chunk), 1) % LANES
    lo = jnp.where(lane < HEAD_DIM, 1.0, 0.0)
    hi = 1.0 - lo
    qscale = LOG2E / math.sqrt(HEAD_DIM)
    for c in range(Q_WIDTH // chunk):
        sl = slice(c * chunk, (c + 1) * chunk)
        oq_ref[:, sl] = (jnp.dot(n, wq_ref[:, sl], preferred_element_type=F32) * qscale).astype(BF16)
        kk = jnp.dot(n, wk_ref[:, sl], preferred_element_type=F32)
        oklo_ref[:, sl] = (kk * lo).astype(BF16)
        okhi_ref[:, sl] = (kk * hi).astype(BF16)
    vd = _nt_dot(wvt_ref[:DIFF_WIDTH, :], n)
    prow = lax.broadcasted_iota(jnp.int32, (BF16_ROWS, TM), 0)
    ones_rows = jnp.where(prow == 0, 1.0, 0.0).astype(BF16)
    for h in range(N_DIFF_HEADS):
        ovd_ref[h * VD_ROWS:h * VD_ROWS + DIFF_V_DIM, :] = (
            vd[h * DIFF_V_DIM:(h + 1) * DIFF_V_DIM].astype(BF16))
        ovd_ref[h * VD_ROWS + DIFF_V_DIM:(h + 1) * VD_ROWS, :] = ones_rows
    vs = _nt_dot(wvt_ref[DIFF_WIDTH:, :], n)
    row = lax.broadcasted_iota(jnp.int32, (SB_WIDTH, 1), 0) % LANES
    ra = jnp.where(row < HEAD_DIM, 1.0, 0.0)
    ovs_ref[:SB_WIDTH, :] = (vs * ra).astype(BF16)
    ovs_ref[SB_WIDTH:, :] = (vs * (1.0 - ra)).astype(BF16)


def _norm_proj(x2, g, w_in, w_up, w_down, w_out):
    rows = x2.shape[0]
    steps = rows // TM
    const = lambda i: (0, 0)
    row_blk = lambda w: pl.BlockSpec((TM, w), lambda i: (i, 0))
    col_blk = lambda r: pl.BlockSpec((r, TM), lambda i: (0, i))
    wu_blk = pl.BlockSpec((D_MODEL, D_FF // steps), lambda i: (0, i))
    wd_blk = pl.BlockSpec((D_FF // steps, D_MODEL), lambda i: (i, 0))
    wo_blk = pl.BlockSpec((D_MODEL // steps, D_MODEL), lambda i: (i, 0))
    return pl.pallas_call(
        _norm_proj_kernel,
        grid=(steps,),
        in_specs=[
            row_blk(D_MODEL),
            pl.BlockSpec((1, D_MODEL), const),
            pl.BlockSpec(w_in.shape, const),
            wu_blk, wd_blk, wo_blk,
        ],
        out_specs=[row_blk(Q_WIDTH), row_blk(Q_WIDTH), row_blk(Q_WIDTH),
                   col_blk(N_DIFF_HEADS * VD_ROWS), col_blk(2 * SB_WIDTH),
                   wu_blk, wd_blk, wo_blk],
        out_shape=[jax.ShapeDtypeStruct((rows, Q_WIDTH), BF16),
                   jax.ShapeDtypeStruct((rows, Q_WIDTH), BF16),
                   jax.ShapeDtypeStruct((rows, Q_WIDTH), BF16),
                   jax.ShapeDtypeStruct((N_DIFF_HEADS * VD_ROWS, rows), BF16),
                   jax.ShapeDtypeStruct((2 * SB_WIDTH, rows), BF16),
                   jax.ShapeDtypeStruct(w_up.shape, BF16),
                   jax.ShapeDtypeStruct(w_down.shape, BF16),
                   jax.ShapeDtypeStruct(w_out.shape, BF16)],
        scratch_shapes=[pltpu.VMEM((D_MODEL, Q_WIDTH), BF16),
                        pltpu.VMEM((D_MODEL, Q_WIDTH), BF16),
                        pltpu.VMEM((Q_WIDTH, D_MODEL), BF16)],
        compiler_params=pltpu.CompilerParams(
            dimension_semantics=("arbitrary",), vmem_limit_bytes=VMEM_LIMIT),
        name="norm_proj",
    )(x2, g, w_in, w_up, w_down, w_out)


def _stage_table(items, lags, fields, defaults):
    n = len(items)
    nsteps = n + max(lags)
    nsteps += nsteps % 2
    rows = [(s, f) for s in range(len(lags)) for f in fields[s]]
    tab = np.zeros((len(rows), nsteps), np.int32)
    for ri, (s, f) in enumerate(rows):
        for t in range(nsteps):
            idx = t - lags[s]
            if 0 <= idx < n:
                tab[ri, t] = items[idx][f]
            elif f in defaults:
                tab[ri, t] = defaults[f]
            else:
                tab[ri, t] = items[min(max(idx, 0), n - 1)][f]
    return tab, {(s, f): ri for ri, (s, f) in enumerate(rows)}


def _diff_schedule(seq):
    items = []
    for i in range(seq // TQ):
        t0 = i * TQ
        items.append(dict(q=t0, k=t0, kind=0, keep=0, dt=0, last=0))
        for j in range(i):
            items.append(dict(q=t0, k=j * TKD, kind=1, keep=1, dt=t0 - j * TKD, last=0))
        items[-1]["last"] = 1
    return _stage_table(items, lags=(0, 1, 2),
                        fields=(("q", "k"), ("kind", "keep", "dt"), ("k", "q", "last")),
                        defaults=dict(keep=1, last=0))


def _sb_schedule(seq):
    items = []
    nd = TQ // TKS
    for i in range(seq // TQ):
        t0 = i * TQ
        first = len(items)
        for d in reversed(range(nd)):
            items.append([t0, t0 + d * TKS, nd - 1 - d, 1, i, 0])
        for j in reversed(range(nd * i)):
            items.append([t0, j * TKS, nd, 1, i, 0])
        items[first][3] = 0
        for it in items[first:]:
            it[5] = len(items)
    return np.asarray(items, np.int32).T.copy()


def _make_diff_kernel(rix):
    heads = range(N_DIFF_HEADS)

    def kernel(tab_ref, slopes_ref, lq1_ref, lk1_ref, lq2_ref, lk2_ref, g_ref, bias_ref,
               q_ref, klo_ref, khi_ref, vd_ref, o_ref,
               s0_scr, s1_scr, p0_scr, p1_scr, al0_scr, al1_scr, m_scr, acc_scr):
        s_scr, p_scr, alpha_scr = (s0_scr, s1_scr), (p0_scr, p1_scr), (al0_scr, al1_scr)
        lam = (jnp.exp(jnp.sum(lq1_ref[...] * lk1_ref[...], axis=1, keepdims=True))
               - jnp.exp(jnp.sum(lq2_ref[...] * lk2_ref[...], axis=1, keepdims=True))
               + LAMBDA_INIT)

        for ref in s_scr + p_scr + alpha_scr:
            ref[...] = jnp.zeros(ref.shape, ref.dtype)
        m_scr[...] = jnp.full(m_scr.shape, NEG_INF, F32)
        acc_scr[...] = jnp.zeros(acc_scr.shape, F32)

        def step(t, cur):
            prev = 1 - cur
            qs = pl.multiple_of(tab_ref[rix[0, "q"], t], TQ)
            ks = pl.multiple_of(tab_ref[rix[0, "k"], t], TKD)
            for h in heads:
                cols = slice(h * LANES, (h + 1) * LANES)
                q = q_ref[0, pl.ds(qs, TQ), cols]
                kbd = jnp.concatenate([klo_ref[0, pl.ds(ks, TKD), cols],
                                       khi_ref[0, pl.ds(ks, TKD), cols]], axis=0)
                s_scr[cur][h] = _nt_dot(kbd, q)

            ks = pl.multiple_of(tab_ref[rix[2, "k"], t], TKD)
            for h in heads:
                vt = vd_ref[h * VD_ROWS:(h + 1) * VD_ROWS, pl.ds(ks, TKD)]
                for m in range(2):
                    row = 2 * h + m
                    pv = jnp.dot(vt, p_scr[prev][row], preferred_element_type=F32)
                    acc_scr[row] = acc_scr[row] * alpha_scr[prev][row:row + 1, :] + pv

            kind = tab_ref[rix[1, "kind"], t]
            keep = tab_ref[rix[1, "keep"], t].astype(F32)
            dt = tab_ref[rix[1, "dt"], t].astype(F32)
            for h in heads:
                shift = -slopes_ref[h] * dt
                for m in range(2):
                    row = 2 * h + m
                    for half in range(TQ // LANES):
                        ls = slice(half * LANES, (half + 1) * LANES)
                        u = s_scr[prev][h, m * TKD:(m + 1) * TKD, ls] + bias_ref[h, kind, :, ls]
                        m_old = m_scr[row:row + 1, ls] * keep + NEG_INF * (1.0 - keep)
                        m_new = jnp.maximum(m_old, jnp.max(u, axis=0, keepdims=True) + shift)
                        p_scr[cur][row, :, ls] = jnp.exp2(u - (m_new - shift)).astype(BF16)
                        alpha_scr[cur][row:row + 1, ls] = jnp.exp2(m_old - m_new)
                        m_scr[row:row + 1, ls] = m_new

            @pl.when(tab_ref[rix[2, "last"], t] == 1)
            def _():
                qf = pl.multiple_of(tab_ref[rix[2, "q"], t], TQ)
                for h in heads:
                    a1 = acc_scr[2 * h]
                    a2 = acc_scr[2 * h + 1]
                    r1 = 1.0 / a1[DIFF_V_DIM:DIFF_V_DIM + 1]
                    r2 = lam / a2[DIFF_V_DIM:DIFF_V_DIM + 1]
                    o = a1[:DIFF_V_DIM] * r1 - a2[:DIFF_V_DIM] * r2
                    ms = jnp.mean(o * o, axis=0, keepdims=True)
                    o = o * lax.rsqrt(ms + EPS) * g_ref[...] * (1.0 - LAMBDA_INIT)
                    o_ref[0, h * LANES:(h + 1) * LANES, pl.ds(qf, TQ)] = o.astype(BF16)

        def two_steps(i, carry):
            step(2 * i, 0)
            step(2 * i + 1, 1)
            return carry

        lax.fori_loop(0, tab_ref.shape[1] // 2, two_steps, 0)

    return kernel


def _diff_attention(q3, klo3, khi3, vd, slopes2, lq1, lk1, lq2, lk2, subln_g_col):
    B, S, _ = q3.shape
    tab, rix = _diff_schedule(S)
    c = np.arange(TKD)[:, None]
    r = np.arange(TQ)[None, :]
    slopes = np.asarray(slopes2, np.float64)[:, None, None]
    diag = np.where((c // CHUNK) <= (r // CHUNK), -slopes * np.abs(r - c), NEG_INF)
    full = -slopes * (r - c)
    bias = np.stack([diag, full], axis=1).astype(np.float32)
    vec = pl.BlockSpec((1, HEAD_DIM), lambda b, tab: (0, 0))
    seq_blk = pl.BlockSpec((1, S, DIFF_WIDTH), lambda b, tab: (b, 0, 0))
    nmap = 2 * N_DIFF_HEADS
    grid_spec = pltpu.PrefetchScalarGridSpec(
        num_scalar_prefetch=1,
        grid=(B,),
        in_specs=[
            pl.BlockSpec(memory_space=pltpu.SMEM),
            vec, vec, vec, vec,
            pl.BlockSpec((DIFF_V_DIM, 1), lambda b, tab: (0, 0)),
            pl.BlockSpec((N_DIFF_HEADS, 2, TKD, TQ), lambda b, tab: (0, 0, 0, 0)),
            seq_blk, seq_blk, seq_blk,
            pl.BlockSpec((N_DIFF_HEADS * VD_ROWS, S), lambda b, tab: (0, b)),
        ],
        out_specs=pl.BlockSpec((1, DIFF_WIDTH, S), lambda b, tab: (b, 0, 0)),
        scratch_shapes=[
            pltpu.VMEM((N_DIFF_HEADS, 2 * TKD, TQ), F32),
            pltpu.VMEM((N_DIFF_HEADS, 2 * TKD, TQ), F32),
            pltpu.VMEM((nmap, TKD, TQ), BF16),
            pltpu.VMEM((nmap, TKD, TQ), BF16),
            pltpu.VMEM((nmap, TQ), F32),
            pltpu.VMEM((nmap, TQ), F32),
            pltpu.VMEM((nmap, TQ), F32),
            pltpu.VMEM((nmap, VD_ROWS, TQ), F32),
        ],
    )
    return pl.pallas_call(
        _make_diff_kernel(rix),
        grid_spec=grid_spec,
        out_shape=jax.ShapeDtypeStruct((B, DIFF_WIDTH, S), BF16),
        compiler_params=pltpu.CompilerParams(
            dimension_semantics=("arbitrary",), vmem_limit_bytes=VMEM_LIMIT),
        name="diff_attn",
    )(jnp.asarray(tab), slopes2, lq1, lk1, lq2, lk2, subln_g_col, jnp.asarray(bias),
      q3, klo3, khi3, vd)


SB_Q, SB_K, SB_MSEL, SB_KEEP, SB_BLK, SB_NEXT = range(6)
SB_DEAD = 160.0


def _stick_break_kernel(tab_ref, g_ref, mask_ref, q_ref, klo_ref, khi_ref, vs_ref, o_ref,
                        z_scr, sp_scr, a_scr, acc_scr, c_scr):
    n_items = tab_ref.shape[1]
    pairs = range(N_SB_PAIRS)
    su = lax.broadcasted_iota(jnp.int32, (2 * TKS, 2 * TKS), 0)
    ju = lax.broadcasted_iota(jnp.int32, (2 * TKS, 2 * TKS), 1)
    utbd = jnp.where((ju >= su) & ((ju // TKS) == (su // TKS)), 1.0, 0.0).astype(BF16)

    z_scr[...] = jnp.full(z_scr.shape, NEG_INF, F32)
    sp_scr[...] = jnp.zeros(sp_scr.shape, BF16)
    a_scr[...] = jnp.zeros(a_scr.shape, BF16)
    acc_scr[...] = jnp.zeros(acc_scr.shape, F32)
    c_scr[...] = jnp.zeros(c_scr.shape, F32)

    def step(carry):
        i1, i2, i3 = carry
        j1 = jnp.minimum(i1, n_items - 1)
        j2 = jnp.minimum(i2, n_items - 1)
        j3 = jnp.minimum(i3, n_items - 1)

        ks = pl.multiple_of(tab_ref[SB_K, j3], TKS)
        for p in pairs:
            rows = slice(p * LANES, (p + 1) * LANES)
            vbd = jnp.concatenate([vs_ref[rows, pl.ds(ks, TKS)],
                                   vs_ref[SB_WIDTH + p * LANES:SB_WIDTH + (p + 1) * LANES,
                                          pl.ds(ks, TKS)]], axis=1)
            acc_scr[p] = acc_scr[p] + jnp.dot(vbd, a_scr[p], preferred_element_type=F32)

        keep2 = tab_ref[SB_KEEP, j2].astype(F32)
        cmin = None
        for p in pairs:
            suf = jnp.dot(utbd, sp_scr[p], preferred_element_type=F32)
            ca = c_scr[2 * p:2 * p + 1, :] * keep2
            cb = c_scr[2 * p + 1:2 * p + 2, :] * keep2
            cpair = jnp.concatenate([jnp.broadcast_to(ca, (TKS, TQ)),
                                     jnp.broadcast_to(cb, (TKS, TQ))], axis=0)
            a_scr[p] = jnp.exp2(z_scr[p] - suf - cpair).astype(BF16)
            ca = ca + suf[0:1, :]
            cb = cb + suf[TKS:TKS + 1, :]
            c_scr[2 * p:2 * p + 1, :] = ca
            c_scr[2 * p + 1:2 * p + 2, :] = cb
            m = jnp.minimum(ca, cb)
            cmin = m if cmin is None else jnp.minimum(cmin, m)
        dead = jnp.min(cmin) >= SB_DEAD

        qs = pl.multiple_of(tab_ref[SB_Q, j1], TQ)
        ks = pl.multiple_of(tab_ref[SB_K, j1], TKS)
        mask_bias = mask_ref[tab_ref[SB_MSEL, j1]]
        for p in pairs:
            cols = slice(p * LANES, (p + 1) * LANES)
            q = q_ref[0, pl.ds(qs, TQ), cols]
            kbd = jnp.concatenate([klo_ref[0, pl.ds(ks, TKS), cols],
                                   khi_ref[0, pl.ds(ks, TKS), cols]], axis=0)
            z = _nt_dot(kbd, q) + mask_bias
            zt = jnp.minimum(z, 126.0)
            sp = jnp.log2(1.0 + jnp.exp2(zt)) + (z - zt)
            z_scr[p] = z
            sp_scr[p] = sp.astype(BF16)

        blk3 = tab_ref[SB_BLK, j3]
        last3 = (i3 < n_items) & ((i2 >= n_items) | (tab_ref[SB_BLK, j2] != blk3))

        @pl.when(last3)
        def _():
            qf = pl.multiple_of(tab_ref[SB_Q, j3], TQ)
            for p in pairs:
                o = acc_scr[p]
                o2 = o * o
                msa = jnp.mean(o2[:HEAD_DIM], axis=0, keepdims=True)
                msb = jnp.mean(o2[HEAD_DIM:], axis=0, keepdims=True)
                ms = jnp.concatenate([jnp.broadcast_to(msa, (HEAD_DIM, TQ)),
                                      jnp.broadcast_to(msb, (HEAD_DIM, TQ))], axis=0)
                o = o * lax.rsqrt(ms + EPS) * g_ref[...]
                o_ref[0, p * LANES:(p + 1) * LANES, pl.ds(qf, TQ)] = o.astype(BF16)
                acc_scr[p] = jnp.zeros((LANES, TQ), F32)

        same_blk = (i2 < n_items) & (tab_ref[SB_BLK, j2] == tab_ref[SB_BLK, j1])
        nxt = jnp.where(dead & same_blk, tab_ref[SB_NEXT, j1], i1 + 1)
        nxt = jnp.where(i1 < n_items, jnp.minimum(nxt, n_items), n_items)
        return nxt, i1, i2

    def busy(carry):
        i1, i2, i3 = carry
        return (i1 < n_items) | (i2 < n_items) | (i3 < n_items)

    none = jnp.int32(n_items)
    lax.while_loop(busy, step, (jnp.int32(0), none, none))


def _stick_breaking(q3, klo3, khi3, vs, g_pair_col):
    B, S, _ = q3.shape
    tab = _sb_schedule(S)
    nd = TQ // TKS
    c = (np.arange(2 * TKS) % TKS)[:, None]
    r = np.arange(TQ)[None, :]
    mask = np.stack([np.where(c + d * TKS < r, 0.0, NEG_INF) for d in reversed(range(nd))]
                    + [np.zeros((2 * TKS, TQ))]).astype(np.float32)
    seq_blk = pl.BlockSpec((1, S, SB_WIDTH), lambda b, tab: (b, 0, 1))
    tile = (N_SB_PAIRS, 2 * TKS, TQ)
    grid_spec = pltpu.PrefetchScalarGridSpec(
        num_scalar_prefetch=1,
        grid=(B,),
        in_specs=[
            pl.BlockSpec((LANES, 1), lambda b, tab: (0, 0)),
            pl.BlockSpec((nd + 1, 2 * TKS, TQ), lambda b, tab: (0, 0, 0)),
            seq_blk, seq_blk, seq_blk,
            pl.BlockSpec((2 * SB_WIDTH, S), lambda b, tab: (0, b)),
        ],
        out_specs=pl.BlockSpec((1, SB_WIDTH, S), lambda b, tab: (b, 0, 0)),
        scratch_shapes=[
            pltpu.VMEM(tile, F32),
            pltpu.VMEM(tile, BF16),
            pltpu.VMEM(tile, BF16),
            pltpu.VMEM((N_SB_PAIRS, LANES, TQ), F32),
            pltpu.VMEM((2 * N_SB_PAIRS, TQ), F32),
        ],
    )
    return pl.pallas_call(
        _stick_break_kernel,
        grid_spec=grid_spec,
        out_shape=jax.ShapeDtypeStruct((B, SB_WIDTH, S), BF16),
        compiler_params=pltpu.CompilerParams(
            dimension_semantics=("arbitrary",), vmem_limit_bytes=VMEM_LIMIT),
        name="stick_break",
    )(jnp.asarray(tab), g_pair_col, jnp.asarray(mask), q3, klo3, khi3, vs)


def _out_ffn_kernel(x_ref, od_ref, os_ref, wo_ref, g2_ref, wu_ref, wd_ref, gf_ref, o_ref):
    tn = (((0,), (0,)), ((), ()))
    h = (x_ref[...]
         + lax.dot_general(od_ref[0], wo_ref[:DIFF_WIDTH, :], tn, preferred_element_type=F32)
         + lax.dot_general(os_ref[0], wo_ref[DIFF_WIDTH:, :], tn, preferred_element_type=F32))
    ms = jnp.mean(h * h, axis=-1, keepdims=True)
    n2 = (h * lax.rsqrt(ms + EPS) * g2_ref[...]).astype(BF16)
    u = jnp.dot(n2, wu_ref[...], preferred_element_type=F32)
    u = jnp.square(jnp.maximum(u, 0.0)).astype(BF16)
    h = h + jnp.dot(u, wd_ref[...], preferred_element_type=F32)
    ms = jnp.mean(h * h, axis=-1, keepdims=True)
    o_ref[...] = h * lax.rsqrt(ms + EPS) * gf_ref[...]


def _out_ffn(x2, od, osb, wo, g2, wu, wd, gf):
    rows = x2.shape[0]
    per_seq = od.shape[2] // TM
    const = lambda i: (0, 0)
    return pl.pallas_call(
        _out_ffn_kernel,
        grid=(rows // TM,),
        in_specs=[
            pl.BlockSpec((TM, D_MODEL), lambda i: (i, 0)),
            pl.BlockSpec((1, DIFF_WIDTH, TM), lambda i: (i // per_seq, 0, i % per_seq)),
            pl.BlockSpec((1, SB_WIDTH, TM), lambda i: (i // per_seq, 0, i % per_seq)),
            pl.BlockSpec((D_MODEL, D_MODEL), const),
            pl.BlockSpec((1, D_MODEL), const),
            pl.BlockSpec((D_MODEL, D_FF), const),
            pl.BlockSpec((D_FF, D_MODEL), const),
            pl.BlockSpec((1, D_MODEL), const),
        ],
        out_specs=pl.BlockSpec((TM, D_MODEL), lambda i: (i, 0)),
        out_shape=jax.ShapeDtypeStruct((rows, D_MODEL), F32),
        compiler_params=pltpu.CompilerParams(
            dimension_semantics=("arbitrary",), vmem_limit_bytes=VMEM_LIMIT),
        name="out_ffn",
    )(x2, od, osb, wo, g2, wu, wd, gf)


def kernel(x, norm1_g, w_in, lambda_q1, lambda_k1, lambda_q2, lambda_k2, diff_subln_g,
           sb_norm_g, w_out, norm2_g, w_up, w_down, final_norm_g):
    B, S, D = x.shape
    x2 = x.reshape(B * S, D)
    slopes2 = (np.exp2(-8.0 * np.arange(1, N_DIFF_HEADS + 1) / N_DIFF_HEADS) * LOG2E).astype(np.float32)

    q, klo, khi, vd, vs, wu, wd, wo = _norm_proj(x2, norm1_g[0][None, :], w_in[0], w_up[0],
                                                 w_down[0], w_out[0])
    q3, klo3, khi3 = (a.reshape(B, S, Q_WIDTH) for a in (q, klo, khi))

    o_diff = _diff_attention(q3, klo3, khi3, vd, slopes2, lambda_q1, lambda_k1, lambda_q2,
                             lambda_k2, diff_subln_g[0][:, None])
    g_pair = jnp.concatenate([sb_norm_g[0], sb_norm_g[0]])[:, None]
    o_sb = _stick_breaking(q3, klo3, khi3, vs, g_pair)

    out = _out_ffn(x2, o_diff, o_sb, wo, norm2_g[0][None, :], wu, wd, final_norm_g[None, :])
    return out.reshape(B, S, D)
```

```python
import math

import numpy as np
import jax
import jax.numpy as jnp
from jax import lax
from jax.experimental import pallas as pl
from jax.experimental.pallas import tpu as pltpu

D_MODEL = 1024
HEAD_DIM = 64
CHUNK = 64
N_DIFF_HEADS = 4
N_SB_HEADS = 8
N_SB_PAIRS = N_SB_HEADS // 2
DIFF_V_DIM = 2 * HEAD_DIM
DIFF_WIDTH = N_DIFF_HEADS * DIFF_V_DIM
SB_WIDTH = N_SB_HEADS * HEAD_DIM
Q_WIDTH = DIFF_WIDTH + SB_WIDTH
D_FF = 4 * D_MODEL
EPS = 1e-6
NEG_INF = -1e30
LAMBDA_INIT = 0.8 - 0.6 * math.exp(-0.3 * 0)
LOG2E = 1.4426950408889634

LANES = 128
BF16_ROWS = 16
TQ = 256
TKD = 256
TKS = 128
NSLAB = TQ // LANES
TM = 512
VD_ROWS = DIFF_V_DIM + BF16_ROWS
VMEM_LIMIT = 56 * 1024 * 1024

F32 = jnp.float32
BF16 = jnp.bfloat16


def _nt_dot(a, b):
    return lax.dot_general(a, b, (((1,), (1,)), ((), ())), preferred_element_type=F32)


def _norm_proj_kernel(x_ref, g_ref, win_ref, wu_ref, wd_ref, wo_ref,
                      oq_ref, oklo_ref, okhi_ref, ovd_ref, ovs_ref, owu_ref, owd_ref, owo_ref,
                      wq_ref, wk_ref, wvt_ref):
    @pl.when(pl.program_id(0) == 0)
    def _():
        d1, d2, d3 = DIFF_WIDTH, 2 * DIFF_WIDTH, 3 * DIFF_WIDTH
        s1, s2 = d3 + SB_WIDTH, d3 + 2 * SB_WIDTH
        wq_ref[:, :DIFF_WIDTH] = win_ref[:, :d1].astype(BF16)
        wq_ref[:, DIFF_WIDTH:] = win_ref[:, d3:s1].astype(BF16)
        wk_ref[:, :DIFF_WIDTH] = win_ref[:, d1:d2].astype(BF16)
        wk_ref[:, DIFF_WIDTH:] = win_ref[:, s1:s2].astype(BF16)
        wvt_ref[:DIFF_WIDTH, :] = win_ref[:, d2:d3].T.astype(BF16)
        wvt_ref[DIFF_WIDTH:, :] = win_ref[:, s2:].T.astype(BF16)

    owu_ref[...] = wu_ref[...].astype(BF16)
    owd_ref[...] = wd_ref[...].astype(BF16)
    owo_ref[...] = wo_ref[...].astype(BF16)

    x = x_ref[...]
    ms = jnp.mean(x * x, axis=-1, keepdims=True)
    n = (x * lax.rsqrt(ms + EPS) * g_ref[...]).astype(BF16)
    chunk = 512
    lane = lax.broadcasted_iota(jnp.int32, (1, chunk), 1) % LANES
    lo = jnp.where(lane < HEAD_DIM, 1.0, 0.0)
    hi = 1.0 - lo
    qscale = LOG2E / math.sqrt(HEAD_DIM)
    for c in range(Q_WIDTH // chunk):
        sl = slice(c * chunk, (c + 1) * chunk)
        oq_ref[:, sl] = (jnp.dot(n, wq_ref[:, sl], preferred_element_type=F32) * qscale).astype(BF16)
        kk = jnp.dot(n, wk_ref[:, sl], preferred_element_type=F32)
        oklo_ref[:, sl] = (kk * lo).astype(BF16)
        okhi_ref[:, sl] = (kk * hi).astype(BF16)
    vd = _nt_dot(wvt_ref[:DIFF_WIDTH, :], n)
    prow = lax.broadcasted_iota(jnp.int32, (BF16_ROWS, TM), 0)
    ones_rows = jnp.where(prow == 0, 1.0, 0.0).astype(BF16)
    for h in range(N_DIFF_HEADS):
        ovd_ref[h * VD_ROWS:h * VD_ROWS + DIFF_V_DIM, :] = (
            vd[h * DIFF_V_DIM:(h + 1) * DIFF_V_DIM].astype(BF16))
        ovd_ref[h * VD_ROWS + DIFF_V_DIM:(h + 1) * VD_ROWS, :] = ones_rows
    vs = _nt_dot(wvt_ref[DIFF_WIDTH:, :], n)
    row = lax.broadcasted_iota(jnp.int32, (SB_WIDTH, 1), 0) % LANES
    ra = jnp.where(row < HEAD_DIM, 1.0, 0.0)
    ovs_ref[:SB_WIDTH, :] = (vs * ra).astype(BF16)
    ovs_ref[SB_WIDTH:, :] = (vs * (1.0 - ra)).astype(BF16)


def _norm_proj(x2, g, w_in, w_up, w_down, w_out):
    rows = x2.shape[0]
    steps = rows // TM
    const = lambda i: (0, 0)
    row_blk = lambda w: pl.BlockSpec((TM, w), lambda i: (i, 0))
    col_blk = lambda r: pl.BlockSpec((r, TM), lambda i: (0, i))
    wu_blk = pl.BlockSpec((D_MODEL, D_FF // steps), lambda i: (0, i))
    wd_blk = pl.BlockSpec((D_FF // steps, D_MODEL), lambda i: (i, 0))
    wo_blk = pl.BlockSpec((D_MODEL // steps, D_MODEL), lambda i: (i, 0))
    return pl.pallas_call(
        _norm_proj_kernel,
        grid=(steps,),
        in_specs=[
            row_blk(D_MODEL),
            pl.BlockSpec((1, D_MODEL), const),
            pl.BlockSpec(w_in.shape, const),
            wu_blk, wd_blk, wo_blk,
        ],
        out_specs=[row_blk(Q_WIDTH), row_blk(Q_WIDTH), row_blk(Q_WIDTH),
                   col_blk(N_DIFF_HEADS * VD_ROWS), col_blk(2 * SB_WIDTH),
                   wu_blk, wd_blk, wo_blk],
        out_shape=[jax.ShapeDtypeStruct((rows, Q_WIDTH), BF16),
                   jax.ShapeDtypeStruct((rows, Q_WIDTH), BF16),
                   jax.ShapeDtypeStruct((rows, Q_WIDTH), BF16),
                   jax.ShapeDtypeStruct((N_DIFF_HEADS * VD_ROWS, rows), BF16),
                   jax.ShapeDtypeStruct((2 * SB_WIDTH, rows), BF16),
                   jax.ShapeDtypeStruct(w_up.shape, BF16),
                   jax.ShapeDtypeStruct(w_down.shape, BF16),
                   jax.ShapeDtypeStruct(w_out.shape, BF16)],
        scratch_shapes=[pltpu.VMEM((D_MODEL, Q_WIDTH), BF16),
                        pltpu.VMEM((D_MODEL, Q_WIDTH), BF16),
                        pltpu.VMEM((Q_WIDTH, D_MODEL), BF16)],
        compiler_params=pltpu.CompilerParams(
            dimension_semantics=("arbitrary",), vmem_limit_bytes=VMEM_LIMIT),
        name="norm_proj",
    )(x2, g, w_in, w_up, w_down, w_out)


def _stage_table(items, lags, fields, defaults):
    n = len(items)
    nsteps = n + max(lags)
    nsteps += nsteps % 2
    rows = [(s, f) for s in range(len(lags)) for f in fields[s]]
    tab = np.zeros((len(rows), nsteps), np.int32)
    for ri, (s, f) in enumerate(rows):
        for t in range(nsteps):
            idx = t - lags[s]
            if 0 <= idx < n:
                tab[ri, t] = items[idx][f]
            elif f in defaults:
                tab[ri, t] = defaults[f]
            else:
                tab[ri, t] = items[min(max(idx, 0), n - 1)][f]
    return tab, {(s, f): ri for ri, (s, f) in enumerate(rows)}


def _diff_schedule(seq):
    items = []
    for i in range(seq // TQ):
        t0 = i * TQ
        items.append(dict(q=t0, k=t0, kind=0, keep=0, dt=0, last=0))
        for j in range(i):
            items.append(dict(q=t0, k=j * TKD, kind=1, keep=1, dt=t0 - j * TKD, last=0))
        items[-1]["last"] = 1
    return _stage_table(items, lags=(0, 1, 2),
                        fields=(("q", "k"), ("kind", "keep", "dt"), ("k", "q", "last")),
                        defaults=dict(keep=1, last=0))


def _sb_schedule(seq):
    items = []
    nd = TQ // TKS
    for i in range(seq // TQ):
        t0 = i * TQ
        first = len(items)
        for d in reversed(range(nd)):
            items.append([t0, t0 + d * TKS, nd - 1 - d, 1, i, 0])
        for j in reversed(range(nd * i)):
            items.append([t0, j * TKS, nd, 1, i, 0])
        items[first][3] = 0
        for it in items[first:]:
            it[5] = len(items)
    return np.asarray(items, np.int32).T.copy()


def _make_diff_kernel(rix):
    heads = range(N_DIFF_HEADS)

    def kernel(tab_ref, slopes_ref, lq1_ref, lk1_ref, lq2_ref, lk2_ref, g_ref, bias_ref,
               q_ref, klo_ref, khi_ref, vd_ref, o_ref,
               s0_scr, s1_scr, p0_scr, p1_scr, al0_scr, al1_scr, m_scr, acc_scr):
        s_scr, p_scr, alpha_scr = (s0_scr, s1_scr), (p0_scr, p1_scr), (al0_scr, al1_scr)
        lam = (jnp.exp(jnp.sum(lq1_ref[...] * lk1_ref[...], axis=1, keepdims=True))
               - jnp.exp(jnp.sum(lq2_ref[...] * lk2_ref[...], axis=1, keepdims=True))
               + LAMBDA_INIT)

        for ref in s_scr + p_scr + alpha_scr:
            ref[...] = jnp.zeros(ref.shape, ref.dtype)
        m_scr[...] = jnp.full(m_scr.shape, NEG_INF, F32)
        acc_scr[...] = jnp.zeros(acc_scr.shape, F32)

        def step(t, cur):
            prev = 1 - cur
            qs = pl.multiple_of(tab_ref[rix[0, "q"], t], TQ)
            ks = pl.multiple_of(tab_ref[rix[0, "k"], t], TKD)
            for h in heads:
                cols = slice(h * LANES, (h + 1) * LANES)
                q = q_ref[0, pl.ds(qs, TQ), cols]
                kbd = jnp.concatenate([klo_ref[0, pl.ds(ks, TKD), cols],
                                       khi_ref[0, pl.ds(ks, TKD), cols]], axis=0)
                s = _nt_dot(kbd, q)
                for half in range(NSLAB):
                    s_scr[cur][h, half] = s[:, half * LANES:(half + 1) * LANES]

            ks = pl.multiple_of(tab_ref[rix[2, "k"], t], TKD)
            for h in heads:
                vt = vd_ref[h * VD_ROWS:(h + 1) * VD_ROWS, pl.ds(ks, TKD)]
                for m in range(2):
                    row = 2 * h + m
                    p = jnp.concatenate([p_scr[prev][row, half] for half in range(NSLAB)], axis=1)
                    pv = jnp.dot(vt, p, preferred_element_type=F32)
                    acc_scr[row] = acc_scr[row] * alpha_scr[prev][row:row + 1, :] + pv

            kind = tab_ref[rix[1, "kind"], t]
            keep = tab_ref[rix[1, "keep"], t].astype(F32)
            dt = tab_ref[rix[1, "dt"], t].astype(F32)
            for h in heads:
                shift = -slopes_ref[h] * dt
                for m in range(2):
                    row = 2 * h + m
                    for half in range(NSLAB):
                        ls = slice(half * LANES, (half + 1) * LANES)
                        u = (s_scr[prev][h, half, m * TKD:(m + 1) * TKD, :]
                             + bias_ref[h, kind, half])
                        m_old = m_scr[row:row + 1, ls] * keep + NEG_INF * (1.0 - keep)
                        m_new = jnp.maximum(m_old, jnp.max(u, axis=0, keepdims=True) + shift)
                        p_scr[cur][row, half] = jnp.exp2(u - (m_new - shift)).astype(BF16)
                        alpha_scr[cur][row:row + 1, ls] = jnp.exp2(m_old - m_new)
                        m_scr[row:row + 1, ls] = m_new

            @pl.when(tab_ref[rix[2, "last"], t] == 1)
            def _():
                qf = pl.multiple_of(tab_ref[rix[2, "q"], t], TQ)
                for h in heads:
                    a1 = acc_scr[2 * h]
                    a2 = acc_scr[2 * h + 1]
                    r1 = 1.0 / a1[DIFF_V_DIM:DIFF_V_DIM + 1]
                    r2 = lam / a2[DIFF_V_DIM:DIFF_V_DIM + 1]
                    o = a1[:DIFF_V_DIM] * r1 - a2[:DIFF_V_DIM] * r2
                    ms = jnp.mean(o * o, axis=0, keepdims=True)
                    o = o * lax.rsqrt(ms + EPS) * g_ref[...] * (1.0 - LAMBDA_INIT)
                    o_ref[0, h * LANES:(h + 1) * LANES, pl.ds(qf, TQ)] = o.astype(BF16)

        def two_steps(i, carry):
            step(2 * i, 0)
            step(2 * i + 1, 1)
            return carry

        lax.fori_loop(0, tab_ref.shape[1] // 2, two_steps, 0)

    return kernel


def _diff_attention(q3, klo3, khi3, vd, slopes2, lq1, lk1, lq2, lk2, subln_g_col):
    B, S, _ = q3.shape
    tab, rix = _diff_schedule(S)
    c = np.arange(TKD)[:, None]
    r = np.arange(TQ)[None, :]
    slopes = np.asarray(slopes2, np.float64)[:, None, None]
    diag = np.where((c // CHUNK) <= (r // CHUNK), -slopes * np.abs(r - c), NEG_INF)
    full = -slopes * (r - c)
    bias = np.stack([diag, full], axis=1).astype(np.float32)
    bias = bias.reshape(N_DIFF_HEADS, 2, TKD, NSLAB, LANES).transpose(0, 1, 3, 2, 4)
    vec = pl.BlockSpec((1, HEAD_DIM), lambda b, tab: (0, 0))
    seq_blk = pl.BlockSpec((1, S, DIFF_WIDTH), lambda b, tab: (b, 0, 0))
    nmap = 2 * N_DIFF_HEADS
    grid_spec = pltpu.PrefetchScalarGridSpec(
        num_scalar_prefetch=1,
        grid=(B,),
        in_specs=[
            pl.BlockSpec(memory_space=pltpu.SMEM),
            vec, vec, vec, vec,
            pl.BlockSpec((DIFF_V_DIM, 1), lambda b, tab: (0, 0)),
            pl.BlockSpec((N_DIFF_HEADS, 2, NSLAB, TKD, LANES), lambda b, tab: (0, 0, 0, 0, 0)),
            seq_blk, seq_blk, seq_blk,
            pl.BlockSpec((N_DIFF_HEADS * VD_ROWS, S), lambda b, tab: (0, b)),
        ],
        out_specs=pl.BlockSpec((1, DIFF_WIDTH, S), lambda b, tab: (b, 0, 0)),
        scratch_shapes=[
            pltpu.VMEM((N_DIFF_HEADS, NSLAB, 2 * TKD, LANES), F32),
            pltpu.VMEM((N_DIFF_HEADS, NSLAB, 2 * TKD, LANES), F32),
            pltpu.VMEM((nmap, NSLAB, TKD, LANES), BF16),
            pltpu.VMEM((nmap, NSLAB, TKD, LANES), BF16),
            pltpu.VMEM((nmap, TQ), F32),
            pltpu.VMEM((nmap, TQ), F32),
            pltpu.VMEM((nmap, TQ), F32),
            pltpu.VMEM((nmap, VD_ROWS, TQ), F32),
        ],
    )
    return pl.pallas_call(
        _make_diff_kernel(rix),
        grid_spec=grid_spec,
        out_shape=jax.ShapeDtypeStruct((B, DIFF_WIDTH, S), BF16),
        compiler_params=pltpu.CompilerParams(
            dimension_semantics=("arbitrary",), vmem_limit_bytes=VMEM_LIMIT),
        name="diff_attn",
    )(jnp.asarray(tab), slopes2, lq1, lk1, lq2, lk2, subln_g_col, jnp.asarray(bias),
      q3, klo3, khi3, vd)


SB_Q, SB_K, SB_MSEL, SB_KEEP, SB_BLK, SB_NEXT = range(6)
SB_DEAD = 160.0


def _stick_break_kernel(tab_ref, g_ref, mask_ref, q_ref, klo_ref, khi_ref, vs_ref, o_ref,
                        z_scr, sp_scr, a_scr, acc_scr, c_scr):
    n_items = tab_ref.shape[1]
    pairs = range(N_SB_PAIRS)
    su = lax.broadcasted_iota(jnp.int32, (2 * TKS, 2 * TKS), 0)
    ju = lax.broadcasted_iota(jnp.int32, (2 * TKS, 2 * TKS), 1)
    utbd = jnp.where((ju >= su) & ((ju // TKS) == (su // TKS)), 1.0, 0.0).astype(BF16)

    z_scr[...] = jnp.full(z_scr.shape, NEG_INF, F32)
    sp_scr[...] = jnp.zeros(sp_scr.shape, BF16)
    a_scr[...] = jnp.zeros(a_scr.shape, BF16)
    acc_scr[...] = jnp.zeros(acc_scr.shape, F32)
    c_scr[...] = jnp.zeros(c_scr.shape, F32)

    def step(carry):
        i1, i2, i3 = carry
        j1 = jnp.minimum(i1, n_items - 1)
        j2 = jnp.minimum(i2, n_items - 1)
        j3 = jnp.minimum(i3, n_items - 1)

        ks = pl.multiple_of(tab_ref[SB_K, j3], TKS)
        for p in pairs:
            rows = slice(p * LANES, (p + 1) * LANES)
            vbd = jnp.concatenate([vs_ref[rows, pl.ds(ks, TKS)],
                                   vs_ref[SB_WIDTH + p * LANES:SB_WIDTH + (p + 1) * LANES,
                                          pl.ds(ks, TKS)]], axis=1)
            acc_scr[p] = acc_scr[p] + jnp.dot(vbd, a_scr[p], preferred_element_type=F32)

        keep2 = tab_ref[SB_KEEP, j2].astype(F32)
        cmin = None
        for p in pairs:
            suf = jnp.dot(utbd, sp_scr[p], preferred_element_type=F32)
            ca = c_scr[2 * p:2 * p + 1, :] * keep2
            cb = c_scr[2 * p + 1:2 * p + 2, :] * keep2
            cpair = jnp.concatenate([jnp.broadcast_to(ca, (TKS, TQ)),
                                     jnp.broadcast_to(cb, (TKS, TQ))], axis=0)
            a_scr[p] = jnp.exp2(z_scr[p] - suf - cpair).astype(BF16)
            ca = ca + suf[0:1, :]
            cb = cb + suf[TKS:TKS + 1, :]
            c_scr[2 * p:2 * p + 1, :] = ca
            c_scr[2 * p + 1:2 * p + 2, :] = cb
            m = jnp.minimum(ca, cb)
            cmin = m if cmin is None else jnp.minimum(cmin, m)
        dead = jnp.min(cmin) >= SB_DEAD

        qs = pl.multiple_of(tab_ref[SB_Q, j1], TQ)
        ks = pl.multiple_of(tab_ref[SB_K, j1], TKS)
        mask_bias = mask_ref[tab_ref[SB_MSEL, j1]]
        for p in pairs:
            cols = slice(p * LANES, (p + 1) * LANES)
            q = q_ref[0, pl.ds(qs, TQ), cols]
            kbd = jnp.concatenate([klo_ref[0, pl.ds(ks, TKS), cols],
                                   khi_ref[0, pl.ds(ks, TKS), cols]], axis=0)
            z = _nt_dot(kbd, q) + mask_bias
            zt = jnp.minimum(z, 126.0)
            sp = jnp.log2(1.0 + jnp.exp2(zt)) + (z - zt)
            z_scr[p] = z
            sp_scr[p] = sp.astype(BF16)

        blk3 = tab_ref[SB_BLK, j3]
        last3 = (i3 < n_items) & ((i2 >= n_items) | (tab_ref[SB_BLK, j2] != blk3))

        @pl.when(last3)
        def _():
            qf = pl.multiple_of(tab_ref[SB_Q, j3], TQ)
            for p in pairs:
                o = acc_scr[p]
                o2 = o * o
                msa = jnp.mean(o2[:HEAD_DIM], axis=0, keepdims=True)
                msb = jnp.mean(o2[HEAD_DIM:], axis=0, keepdims=True)
                ms = jnp.concatenate([jnp.broadcast_to(msa, (HEAD_DIM, TQ)),
                                      jnp.broadcast_to(msb, (HEAD_DIM, TQ))], axis=0)
                o = o * lax.rsqrt(ms + EPS) * g_ref[...]
                o_ref[0, p * LANES:(p + 1) * LANES, pl.ds(qf, TQ)] = o.astype(BF16)
                acc_scr[p] = jnp.zeros((LANES, TQ), F32)

        same_blk = (i2 < n_items) & (tab_ref[SB_BLK, j2] == tab_ref[SB_BLK, j1])
        nxt = jnp.where(dead & same_blk, tab_ref[SB_NEXT, j1], i1 + 1)
        nxt = jnp.where(i1 < n_items, jnp.minimum(nxt, n_items), n_items)
        return nxt, i1, i2

    def busy(carry):
        i1, i2, i3 = carry
        return (i1 < n_items) | (i2 < n_items) | (i3 < n_items)

    none = jnp.int32(n_items)
    lax.while_loop(busy, step, (jnp.int32(0), none, none))


def _stick_breaking(q3, klo3, khi3, vs, g_pair_col):
    B, S, _ = q3.shape
    tab = _sb_schedule(S)
    nd = TQ // TKS
    c = (np.arange(2 * TKS) % TKS)[:, None]
    r = np.arange(TQ)[None, :]
    mask = np.stack([np.where(c + d * TKS < r, 0.0, NEG_INF) for d in reversed(range(nd))]
                    + [np.zeros((2 * TKS, TQ))]).astype(np.float32)
    seq_blk = pl.BlockSpec((1, S, SB_WIDTH), lambda b, tab: (b, 0, 1))
    tile = (N_SB_PAIRS, 2 * TKS, TQ)
    grid_spec = pltpu.PrefetchScalarGridSpec(
        num_scalar_prefetch=1,
        grid=(B,),
        in_specs=[
            pl.BlockSpec((LANES, 1), lambda b, tab: (0, 0)),
            pl.BlockSpec((nd + 1, 2 * TKS, TQ), lambda b, tab: (0, 0, 0)),
            seq_blk, seq_blk, seq_blk,
            pl.BlockSpec((2 * SB_WIDTH, S), lambda b, tab: (0, b)),
        ],
        out_specs=pl.BlockSpec((1, SB_WIDTH, S), lambda b, tab: (b, 0, 0)),
        scratch_shapes=[
            pltpu.VMEM(tile, F32),
            pltpu.VMEM(tile, BF16),
            pltpu.VMEM(tile, BF16),
            pltpu.VMEM((N_SB_PAIRS, LANES, TQ), F32),
            pltpu.VMEM((2 * N_SB_PAIRS, TQ), F32),
        ],
    )
    return pl.pallas_call(
        _stick_break_kernel,
        grid_spec=grid_spec,
        out_shape=jax.ShapeDtypeStruct((B, SB_WIDTH, S), BF16),
        compiler_params=pltpu.CompilerParams(
            dimension_semantics=("arbitrary",), vmem_limit_bytes=VMEM_LIMIT),
        name="stick_break",
    )(jnp.asarray(tab), g_pair_col, jnp.asarray(mask), q3, klo3, khi3, vs)


def _out_ffn_kernel(x_ref, od_ref, os_ref, wo_ref, g2_ref, wu_ref, wd_ref, gf_ref, o_ref):
    tn = (((0,), (0,)), ((), ()))
    h = (x_ref[...]
         + lax.dot_general(od_ref[0], wo_ref[:DIFF_WIDTH, :], tn, preferred_element_type=F32)
         + lax.dot_general(os_ref[0], wo_ref[DIFF_WIDTH:, :], tn, preferred_element_type=F32))
    ms = jnp.mean(h * h, axis=-1, keepdims=True)
    n2 = (h * lax.rsqrt(ms + EPS) * g2_ref[...]).astype(BF16)
    u = jnp.dot(n2, wu_ref[...], preferred_element_type=F32)
    u = jnp.square(jnp.maximum(u, 0.0)).astype(BF16)
    h = h + jnp.dot(u, wd_ref[...], preferred_element_type=F32)
    ms = jnp.mean(h * h, axis=-1, keepdims=True)
    o_ref[...] = h * lax.rsqrt(ms + EPS) * gf_ref[...]


def _out_ffn(x2, od, osb, wo, g2, wu, wd, gf):
    rows = x2.shape[0]
    per_seq = od.shape[2] // TM
    const = lambda i: (0, 0)
    return pl.pallas_call(
        _out_ffn_kernel,
        grid=(rows // TM,),
        in_specs=[
            pl.BlockSpec((TM, D_MODEL), lambda i: (i, 0)),
            pl.BlockSpec((1, DIFF_WIDTH, TM), lambda i: (i // per_seq, 0, i % per_seq)),
            pl.BlockSpec((1, SB_WIDTH, TM), lambda i: (i // per_seq, 0, i % per_seq)),
            pl.BlockSpec((D_MODEL, D_MODEL), const),
            pl.BlockSpec((1, D_MODEL), const),
            pl.BlockSpec((D_MODEL, D_FF), const),
            pl.BlockSpec((D_FF, D_MODEL), const),
            pl.BlockSpec((1, D_MODEL), const),
        ],
        out_specs=pl.BlockSpec((TM, D_MODEL), lambda i: (i, 0)),
        out_shape=jax.ShapeDtypeStruct((rows, D_MODEL), F32),
        compiler_params=pltpu.CompilerParams(
            dimension_semantics=("arbitrary",), vmem_limit_bytes=VMEM_LIMIT),
        name="out_ffn",
    )(x2, od, osb, wo, g2, wu, wd, gf)


def kernel(x, norm1_g, w_in, lambda_q1, lambda_k1, lambda_q2, lambda_k2, diff_subln_g,
           sb_norm_g, w_out, norm2_g, w_up, w_down, final_norm_g):
    B, S, D = x.shape
    x2 = x.reshape(B * S, D)
    slopes2 = (np.exp2(-8.0 * np.arange(1, N_DIFF_HEADS + 1) / N_DIFF_HEADS) * LOG2E).astype(np.float32)

    q, klo, khi, vd, vs, wu, wd, wo = _norm_proj(x2, norm1_g[0][None, :], w_in[0], w_up[0],
                                                 w_down[0], w_out[0])
    q3, klo3, khi3 = (a.reshape(B, S, Q_WIDTH) for a in (q, klo, khi))

    o_diff = _diff_attention(q3, klo3, khi3, vd, slopes2, lambda_q1, lambda_k1, lambda_q2,
                             lambda_k2, diff_subln_g[0][:, None])
    g_pair = jnp.concatenate([sb_norm_g[0], sb_norm_g[0]])[:, None]
    o_sb = _stick_breaking(q3, klo3, khi3, vs, g_pair)

    out = _out_ffn(x2, o_diff, o_sb, wo, norm2_g[0][None, :], wu, wd, final_norm_g[None, :])
    return out.reshape(B, S, D)
```

```python
import math

import numpy as np
import jax
import jax.numpy as jnp
from jax import lax
from jax.experimental import pallas as pl
from jax.experimental.pallas import tpu as pltpu

D_MODEL = 1024
HEAD_DIM = 64
CHUNK = 64
N_DIFF_HEADS = 4
N_SB_HEADS = 8
N_SB_PAIRS = N_SB_HEADS // 2
DIFF_V_DIM = 2 * HEAD_DIM
DIFF_WIDTH = N_DIFF_HEADS * DIFF_V_DIM
SB_WIDTH = N_SB_HEADS * HEAD_DIM
Q_WIDTH = DIFF_WIDTH + SB_WIDTH
D_FF = 4 * D_MODEL
EPS = 1e-6
NEG_INF = -1e30
LAMBDA_INIT = 0.8 - 0.6 * math.exp(-0.3 * 0)
LOG2E = 1.4426950408889634

LANES = 128
BF16_ROWS = 16
TQ = 256
TKD = 256
TKS = 128
NSLAB = TQ // LANES
TM = 512
VD_ROWS = DIFF_V_DIM + BF16_ROWS
VMEM_LIMIT = 56 * 1024 * 1024

F32 = jnp.float32
BF16 = jnp.bfloat16


def _nt_dot(a, b):
    return lax.dot_general(a, b, (((1,), (1,)), ((), ())), preferred_element_type=F32)


def _norm_proj_kernel(x_ref, g_ref, win_ref, wu_ref, wd_ref, wo_ref,
                      oq_ref, oklo_ref, okhi_ref, ovd_ref, ovs_ref, owu_ref, owd_ref, owo_ref,
                      wq_ref, wk_ref, wvt_ref):
    @pl.when(pl.program_id(0) == 0)
    def _():
        d1, d2, d3 = DIFF_WIDTH, 2 * DIFF_WIDTH, 3 * DIFF_WIDTH
        s1, s2 = d3 + SB_WIDTH, d3 + 2 * SB_WIDTH
        wq_ref[:, :DIFF_WIDTH] = win_ref[:, :d1].astype(BF16)
        wq_ref[:, DIFF_WIDTH:] = win_ref[:, d3:s1].astype(BF16)
        wk_ref[:, :DIFF_WIDTH] = win_ref[:, d1:d2].astype(BF16)
        wk_ref[:, DIFF_WIDTH:] = win_ref[:, s1:s2].astype(BF16)
        wvt_ref[:DIFF_WIDTH, :] = win_ref[:, d2:d3].T.astype(BF16)
        wvt_ref[DIFF_WIDTH:, :] = win_ref[:, s2:].T.astype(BF16)

    owu_ref[...] = wu_ref[...].astype(BF16)
    owd_ref[...] = wd_ref[...].astype(BF16)
    owo_ref[...] = wo_ref[...].astype(BF16)

    x = x_ref[...]
    ms = jnp.mean(x * x, axis=-1, keepdims=True)
    n = (x * lax.rsqrt(ms + EPS) * g_ref[...]).astype(BF16)
    chunk = 512
    lane = lax.broadcasted_iota(jnp.int32, (1, chunk), 1) % LANES
    lo = jnp.where(lane < HEAD_DIM, 1.0, 0.0)
    hi = 1.0 - lo
    qscale = LOG2E / math.sqrt(HEAD_DIM)
    for c in range(Q_WIDTH // chunk):
        sl = slice(c * chunk, (c + 1) * chunk)
        oq_ref[:, sl] = (jnp.dot(n, wq_ref[:, sl], preferred_element_type=F32) * qscale).astype(BF16)
        kk = jnp.dot(n, wk_ref[:, sl], preferred_element_type=F32)
        oklo_ref[:, sl] = (kk * lo).astype(BF16)
        okhi_ref[:, sl] = (kk * hi).astype(BF16)
    vd = _nt_dot(wvt_ref[:DIFF_WIDTH, :], n)
    prow = lax.broadcasted_iota(jnp.int32, (BF16_ROWS, TM), 0)
    ones_rows = jnp.where(prow == 0, 1.0, 0.0).astype(BF16)
    for h in range(N_DIFF_HEADS):
        ovd_ref[h * VD_ROWS:h * VD_ROWS + DIFF_V_DIM, :] = (
            vd[h * DIFF_V_DIM:(h + 1) * DIFF_V_DIM].astype(BF16))
        ovd_ref[h * VD_ROWS + DIFF_V_DIM:(h + 1) * VD_ROWS, :] = ones_rows
    vs = _nt_dot(wvt_ref[DIFF_WIDTH:, :], n)
    row = lax.broadcasted_iota(jnp.int32, (SB_WIDTH, 1), 0) % LANES
    ra = jnp.where(row < HEAD_DIM, 1.0, 0.0)
    ovs_ref[:SB_WIDTH, :] = (vs * ra).astype(BF16)
    ovs_ref[SB_WIDTH:, :] = (vs * (1.0 - ra)).astype(BF16)


def _norm_proj(x2, g, w_in, w_up, w_down, w_out):
    rows = x2.shape[0]
    steps = rows // TM
    const = lambda i: (0, 0)
    row_blk = lambda w: pl.BlockSpec((TM, w), lambda i: (i, 0))
    col_blk = lambda r: pl.BlockSpec((r, TM), lambda i: (0, i))
    wu_blk = pl.BlockSpec((D_MODEL, D_FF // steps), lambda i: (0, i))
    wd_blk = pl.BlockSpec((D_FF // steps, D_MODEL), lambda i: (i, 0))
    wo_blk = pl.BlockSpec((D_MODEL // steps, D_MODEL), lambda i: (i, 0))
    return pl.pallas_call(
        _norm_proj_kernel,
        grid=(steps,),
        in_specs=[
            row_blk(D_MODEL),
            pl.BlockSpec((1, D_MODEL), const),
            pl.BlockSpec(w_in.shape, const),
            wu_blk, wd_blk, wo_blk,
        ],
        out_specs=[row_blk(Q_WIDTH), row_blk(Q_WIDTH), row_blk(Q_WIDTH),
                   col_blk(N_DIFF_HEADS * VD_ROWS), col_blk(2 * SB_WIDTH),
                   wu_blk, wd_blk, wo_blk],
        out_shape=[jax.ShapeDtypeStruct((rows, Q_WIDTH), BF16),
                   jax.ShapeDtypeStruct((rows, Q_WIDTH), BF16),
                   jax.ShapeDtypeStruct((rows, Q_WIDTH), BF16),
                   jax.ShapeDtypeStruct((N_DIFF_HEADS * VD_ROWS, rows), BF16),
                   jax.ShapeDtypeStruct((2 * SB_WIDTH, rows), BF16),
                   jax.ShapeDtypeStruct(w_up.shape, BF16),
                   jax.ShapeDtypeStruct(w_down.shape, BF16),
                   jax.ShapeDtypeStruct(w_out.shape, BF16)],
        scratch_shapes=[pltpu.VMEM((D_MODEL, Q_WIDTH), BF16),
                        pltpu.VMEM((D_MODEL, Q_WIDTH), BF16),
                        pltpu.VMEM((Q_WIDTH, D_MODEL), BF16)],
        compiler_params=pltpu.CompilerParams(
            dimension_semantics=("arbitrary",), vmem_limit_bytes=VMEM_LIMIT),
        name="norm_proj",
    )(x2, g, w_in, w_up, w_down, w_out)


def _stage_table(items, lags, fields, defaults):
    n = len(items)
    nsteps = n + max(lags)
    nsteps += nsteps % 2
    rows = [(s, f) for s in range(len(lags)) for f in fields[s]]
    tab = np.zeros((len(rows), nsteps), np.int32)
    for ri, (s, f) in enumerate(rows):
        for t in range(nsteps):
            idx = t - lags[s]
            if 0 <= idx < n:
                tab[ri, t] = items[idx][f]
            elif f in defaults:
                tab[ri, t] = defaults[f]
            else:
                tab[ri, t] = items[min(max(idx, 0), n - 1)][f]
    return tab, {(s, f): ri for ri, (s, f) in enumerate(rows)}


def _diff_schedule(seq):
    items = []
    for i in range(seq // TQ):
        t0 = i * TQ
        items.append(dict(q=t0, k=t0, kind=0, keep=0, dt=0, last=0))
        for j in range(i):
            items.append(dict(q=t0, k=j * TKD, kind=1, keep=1, dt=t0 - j * TKD, last=0))
        items[-1]["last"] = 1
    return _stage_table(items, lags=(0, 1, 2),
                        fields=(("q", "k"), ("kind", "keep", "dt"), ("k", "q", "last")),
                        defaults=dict(keep=1, last=0))


def _sb_schedule(seq):
    items = []
    nd = TQ // TKS
    for i in range(seq // TQ):
        t0 = i * TQ
        first = len(items)
        for d in reversed(range(nd)):
            items.append([t0, t0 + d * TKS, nd - 1 - d, 1, i, 0])
        for j in reversed(range(nd * i)):
            items.append([t0, j * TKS, nd, 1, i, 0])
        items[first][3] = 0
        for it in items[first:]:
            it[5] = len(items)
    return np.asarray(items, np.int32).T.copy()


def _make_diff_kernel(rix):
    heads = range(N_DIFF_HEADS)

    def kernel(tab_ref, slopes_ref, lq1_ref, lk1_ref, lq2_ref, lk2_ref, g_ref, bias_ref,
               q_ref, klo_ref, khi_ref, vd_ref, o_ref,
               s0_scr, s1_scr, p0_scr, p1_scr, al0_scr, al1_scr, m_scr, acc_scr):
        s_scr, p_scr, alpha_scr = (s0_scr, s1_scr), (p0_scr, p1_scr), (al0_scr, al1_scr)
        lam = (jnp.exp(jnp.sum(lq1_ref[...] * lk1_ref[...], axis=1, keepdims=True))
               - jnp.exp(jnp.sum(lq2_ref[...] * lk2_ref[...], axis=1, keepdims=True))
               + LAMBDA_INIT)

        for ref in s_scr + p_scr + alpha_scr:
            ref[...] = jnp.zeros(ref.shape, ref.dtype)
        m_scr[...] = jnp.full(m_scr.shape, NEG_INF, F32)
        acc_scr[...] = jnp.zeros(acc_scr.shape, F32)

        def step(t, cur):
            prev = 1 - cur
            qs = pl.multiple_of(tab_ref[rix[0, "q"], t], TQ)
            ks = pl.multiple_of(tab_ref[rix[0, "k"], t], TKD)
            for h in heads:
                cols = slice(h * LANES, (h + 1) * LANES)
                q = q_ref[0, pl.ds(qs, TQ), cols]
                kbd = jnp.concatenate([klo_ref[0, pl.ds(ks, TKD), cols],
                                       khi_ref[0, pl.ds(ks, TKD), cols]], axis=0)
                s = _nt_dot(kbd, q)
                for half in range(NSLAB):
                    s_scr[cur][h, half] = s[:, half * LANES:(half + 1) * LANES]

            ks = pl.multiple_of(tab_ref[rix[2, "k"], t], TKD)
            for h in heads:
                vt = vd_ref[h * VD_ROWS:(h + 1) * VD_ROWS, pl.ds(ks, TKD)]
                for m in range(2):
                    row = 2 * h + m
                    p = jnp.concatenate([p_scr[prev][row, half] for half in range(NSLAB)], axis=1)
                    pv = jnp.dot(vt, p, preferred_element_type=F32)
                    acc_scr[row] = acc_scr[row] * alpha_scr[prev][row:row + 1, :] + pv

            kind = tab_ref[rix[1, "kind"], t]
            keep = tab_ref[rix[1, "keep"], t].astype(F32)
            dt = tab_ref[rix[1, "dt"], t].astype(F32)
            for h in heads:
                shift = -slopes_ref[h] * dt
                for m in range(2):
                    row = 2 * h + m
                    for half in range(NSLAB):
                        ls = slice(half * LANES, (half + 1) * LANES)
                        u = (s_scr[prev][h, half, m * TKD:(m + 1) * TKD, :]
                             + bias_ref[h, kind, half])
                        m_old = m_scr[row:row + 1, ls] * keep + NEG_INF * (1.0 - keep)
                        m_new = jnp.maximum(m_old, jnp.max(u, axis=0, keepdims=True) + shift)
                        p_scr[cur][row, half] = jnp.exp2(u - (m_new - shift)).astype(BF16)
                        alpha_scr[cur][row:row + 1, ls] = jnp.exp2(m_old - m_new)
                        m_scr[row:row + 1, ls] = m_new

            @pl.when(tab_ref[rix[2, "last"], t] == 1)
            def _():
                qf = pl.multiple_of(tab_ref[rix[2, "q"], t], TQ)
                for h in heads:
                    a1 = acc_scr[2 * h]
                    a2 = acc_scr[2 * h + 1]
                    r1 = 1.0 / a1[DIFF_V_DIM:DIFF_V_DIM + 1]
                    r2 = lam / a2[DIFF_V_DIM:DIFF_V_DIM + 1]
                    o = a1[:DIFF_V_DIM] * r1 - a2[:DIFF_V_DIM] * r2
                    ms = jnp.mean(o * o, axis=0, keepdims=True)
                    o = o * lax.rsqrt(ms + EPS) * g_ref[...] * (1.0 - LAMBDA_INIT)
                    o_ref[0, h * LANES:(h + 1) * LANES, pl.ds(qf, TQ)] = o.astype(BF16)

        def two_steps(i, carry):
            step(2 * i, 0)
            step(2 * i + 1, 1)
            return carry

        lax.fori_loop(0, tab_ref.shape[1] // 2, two_steps, 0)

    return kernel


def _diff_attention(q3, klo3, khi3, vd, slopes2, lq1, lk1, lq2, lk2, subln_g_col):
    B, S, _ = q3.shape
    tab, rix = _diff_schedule(S)
    c = np.arange(TKD)[:, None]
    r = np.arange(TQ)[None, :]
    slopes = np.asarray(slopes2, np.float64)[:, None, None]
    diag = np.where((c // CHUNK) <= (r // CHUNK), -slopes * np.abs(r - c), NEG_INF)
    full = -slopes * (r - c)
    bias = np.stack([diag, full], axis=1).astype(np.float32)
    bias = bias.reshape(N_DIFF_HEADS, 2, TKD, NSLAB, LANES).transpose(0, 1, 3, 2, 4)
    vec = pl.BlockSpec((1, HEAD_DIM), lambda b, tab: (0, 0))
    seq_blk = pl.BlockSpec((1, S, DIFF_WIDTH), lambda b, tab: (b, 0, 0))
    nmap = 2 * N_DIFF_HEADS
    grid_spec = pltpu.PrefetchScalarGridSpec(
        num_scalar_prefetch=1,
        grid=(B,),
        in_specs=[
            pl.BlockSpec(memory_space=pltpu.SMEM),
            vec, vec, vec, vec,
            pl.BlockSpec((DIFF_V_DIM, 1), lambda b, tab: (0, 0)),
            pl.BlockSpec((N_DIFF_HEADS, 2, NSLAB, TKD, LANES), lambda b, tab: (0, 0, 0, 0, 0)),
            seq_blk, seq_blk, seq_blk,
            pl.BlockSpec((N_DIFF_HEADS * VD_ROWS, S), lambda b, tab: (0, b)),
        ],
        out_specs=pl.BlockSpec((1, DIFF_WIDTH, S), lambda b, tab: (b, 0, 0)),
        scratch_shapes=[
            pltpu.VMEM((N_DIFF_HEADS, NSLAB, 2 * TKD, LANES), F32),
            pltpu.VMEM((N_DIFF_HEADS, NSLAB, 2 * TKD, LANES), F32),
            pltpu.VMEM((nmap, NSLAB, TKD, LANES), BF16),
            pltpu.VMEM((nmap, NSLAB, TKD, LANES), BF16),
            pltpu.VMEM((nmap, TQ), F32),
            pltpu.VMEM((nmap, TQ), F32),
            pltpu.VMEM((nmap, TQ), F32),
            pltpu.VMEM((nmap, VD_ROWS, TQ), F32),
        ],
    )
    return pl.pallas_call(
        _make_diff_kernel(rix),
        grid_spec=grid_spec,
        out_shape=jax.ShapeDtypeStruct((B, DIFF_WIDTH, S), BF16),
        compiler_params=pltpu.CompilerParams(
            dimension_semantics=("arbitrary",), vmem_limit_bytes=VMEM_LIMIT),
        name="diff_attn",
    )(jnp.asarray(tab), slopes2, lq1, lk1, lq2, lk2, subln_g_col, jnp.asarray(bias),
      q3, klo3, khi3, vd)


SB_Q, SB_K, SB_MSEL, SB_KEEP, SB_BLK, SB_NEXT = range(6)
SB_DEAD = 160.0


def _stick_break_kernel(tab_ref, g_ref, mask_ref, stair_ref, q_ref, klo_ref, khi_ref, vs_ref, o_ref,
                        z_scr, sp_scr, a_scr, acc_scr, c_scr):
    n_items = tab_ref.shape[1]
    pairs = range(N_SB_PAIRS)
    su = lax.broadcasted_iota(jnp.int32, (2 * TKS, 2 * TKS), 0)
    ju = lax.broadcasted_iota(jnp.int32, (2 * TKS, 2 * TKS), 1)
    utbd = jnp.where((ju >= su) & ((ju // TKS) == (su // TKS)), 1.0, 0.0).astype(BF16)

    z_scr[...] = jnp.full(z_scr.shape, NEG_INF, F32)
    sp_scr[...] = jnp.zeros(sp_scr.shape, BF16)
    a_scr[...] = jnp.zeros(a_scr.shape, BF16)
    acc_scr[...] = jnp.zeros(acc_scr.shape, F32)
    c_scr[...] = jnp.zeros(c_scr.shape, F32)

    def step(carry):
        i1, i2, i3 = carry
        j1 = jnp.minimum(i1, n_items - 1)
        j2 = jnp.minimum(i2, n_items - 1)
        j3 = jnp.minimum(i3, n_items - 1)

        ks = pl.multiple_of(tab_ref[SB_K, j3], TKS)
        for p in pairs:
            rows = slice(p * LANES, (p + 1) * LANES)
            vbd = jnp.concatenate([vs_ref[rows, pl.ds(ks, TKS)],
                                   vs_ref[SB_WIDTH + p * LANES:SB_WIDTH + (p + 1) * LANES,
                                          pl.ds(ks, TKS)]], axis=1)
            acc_scr[p] = acc_scr[p] + jnp.dot(vbd, a_scr[p], preferred_element_type=F32)

        keep2 = tab_ref[SB_KEEP, j2].astype(F32)
        cmin = None
        for p in pairs:
            suf = jnp.dot(utbd, sp_scr[p], preferred_element_type=F32)
            ca = c_scr[2 * p:2 * p + 1, :] * keep2
            cb = c_scr[2 * p + 1:2 * p + 2, :] * keep2
            cpair = jnp.concatenate([jnp.broadcast_to(ca, (TKS, TQ)),
                                     jnp.broadcast_to(cb, (TKS, TQ))], axis=0)
            a_scr[p] = jnp.exp2((z_scr[p] - suf - cpair).astype(BF16))
            ca = ca + suf[0:1, :]
            cb = cb + suf[TKS:TKS + 1, :]
            c_scr[2 * p:2 * p + 1, :] = ca
            c_scr[2 * p + 1:2 * p + 2, :] = cb
            m = jnp.minimum(ca, cb)
            cmin = m if cmin is None else jnp.minimum(cmin, m)
        dead = jnp.min(cmin) >= SB_DEAD

        qs = pl.multiple_of(tab_ref[SB_Q, j1], TQ)
        ks = pl.multiple_of(tab_ref[SB_K, j1], TKS)
        qcode = mask_ref[tab_ref[SB_MSEL, j1]]
        kstair = stair_ref[...]
        for p in pairs:
            cols = slice(p * LANES, (p + 1) * LANES)
            q = jnp.concatenate([q_ref[0, pl.ds(qs, TQ), cols], qcode], axis=1)
            kbd = jnp.concatenate([klo_ref[0, pl.ds(ks, TKS), cols],
                                   khi_ref[0, pl.ds(ks, TKS), cols]], axis=0)
            kbd = jnp.concatenate([kbd, kstair], axis=1)
            z = _nt_dot(kbd, q)
            zt = jnp.minimum(z, 126.0)
            sp = jnp.log2(1.0 + jnp.exp2(zt)) + (z - zt)
            z_scr[p] = z
            sp_scr[p] = sp.astype(BF16)

        blk3 = tab_ref[SB_BLK, j3]
        last3 = (i3 < n_items) & ((i2 >= n_items) | (tab_ref[SB_BLK, j2] != blk3))

        @pl.when(last3)
        def _():
            qf = pl.multiple_of(tab_ref[SB_Q, j3], TQ)
            for p in pairs:
                o = acc_scr[p]
                o2 = o * o
                msa = jnp.mean(o2[:HEAD_DIM], axis=0, keepdims=True)
                msb = jnp.mean(o2[HEAD_DIM:], axis=0, keepdims=True)
                ms = jnp.concatenate([jnp.broadcast_to(msa, (HEAD_DIM, TQ)),
                                      jnp.broadcast_to(msb, (HEAD_DIM, TQ))], axis=0)
                o = o * lax.rsqrt(ms + EPS) * g_ref[...]
                o_ref[0, p * LANES:(p + 1) * LANES, pl.ds(qf, TQ)] = o.astype(BF16)
                acc_scr[p] = jnp.zeros((LANES, TQ), F32)

        same_blk = (i2 < n_items) & (tab_ref[SB_BLK, j2] == tab_ref[SB_BLK, j1])
        nxt = jnp.where(dead & same_blk, tab_ref[SB_NEXT, j1], i1 + 1)
        nxt = jnp.where(i1 < n_items, jnp.minimum(nxt, n_items), n_items)
        return nxt, i1, i2

    def busy(carry):
        i1, i2, i3 = carry
        return (i1 < n_items) | (i2 < n_items) | (i3 < n_items)

    none = jnp.int32(n_items)
    lax.while_loop(busy, step, (jnp.int32(0), none, none))


def _stick_breaking(q3, klo3, khi3, vs, g_pair_col):
    B, S, _ = q3.shape
    tab = _sb_schedule(S)
    nd = TQ // TKS
    r = np.arange(TQ)[:, None]
    x = np.arange(TKS)[None, :]
    codes = []
    for d in reversed(range(nd)):
        rel = r - d * TKS
        codes.append(((np.maximum(rel, 0) == x) & (rel < TKS)).astype(np.float32))
    codes.append(np.zeros((TQ, TKS), np.float32))
    mask = jnp.asarray(np.stack(codes), BF16)
    c = (np.arange(2 * TKS) % TKS)[:, None]
    stair = jnp.asarray(np.where(c >= x, NEG_INF, 0.0), BF16)
    seq_blk = pl.BlockSpec((1, S, SB_WIDTH), lambda b, tab: (b, 0, 1))
    tile = (N_SB_PAIRS, 2 * TKS, TQ)
    grid_spec = pltpu.PrefetchScalarGridSpec(
        num_scalar_prefetch=1,
        grid=(B,),
        in_specs=[
            pl.BlockSpec((LANES, 1), lambda b, tab: (0, 0)),
            pl.BlockSpec((nd + 1, TQ, TKS), lambda b, tab: (0, 0, 0)),
            pl.BlockSpec((2 * TKS, TKS), lambda b, tab: (0, 0)),
            seq_blk, seq_blk, seq_blk,
            pl.BlockSpec((2 * SB_WIDTH, S), lambda b, tab: (0, b)),
        ],
        out_specs=pl.BlockSpec((1, SB_WIDTH, S), lambda b, tab: (b, 0, 0)),
        scratch_shapes=[
            pltpu.VMEM(tile, F32),
            pltpu.VMEM(tile, BF16),
            pltpu.VMEM(tile, BF16),
            pltpu.VMEM((N_SB_PAIRS, LANES, TQ), F32),
            pltpu.VMEM((2 * N_SB_PAIRS, TQ), F32),
        ],
    )
    return pl.pallas_call(
        _stick_break_kernel,
        grid_spec=grid_spec,
        out_shape=jax.ShapeDtypeStruct((B, SB_WIDTH, S), BF16),
        compiler_params=pltpu.CompilerParams(
            dimension_semantics=("arbitrary",), vmem_limit_bytes=VMEM_LIMIT),
        name="stick_break",
    )(jnp.asarray(tab), g_pair_col, mask, stair, q3, klo3, khi3, vs)


def _out_ffn_kernel(x_ref, od_ref, os_ref, wo_ref, g2_ref, wu_ref, wd_ref, gf_ref, o_ref):
    tn = (((0,), (0,)), ((), ()))
    h = (x_ref[...]
         + lax.dot_general(od_ref[0], wo_ref[:DIFF_WIDTH, :], tn, preferred_element_type=F32)
         + lax.dot_general(os_ref[0], wo_ref[DIFF_WIDTH:, :], tn, preferred_element_type=F32))
    ms = jnp.mean(h * h, axis=-1, keepdims=True)
    n2 = (h * lax.rsqrt(ms + EPS) * g2_ref[...]).astype(BF16)
    u = jnp.dot(n2, wu_ref[...], preferred_element_type=F32)
    u = jnp.square(jnp.maximum(u, 0.0)).astype(BF16)
    h = h + jnp.dot(u, wd_ref[...], preferred_element_type=F32)
    ms = jnp.mean(h * h, axis=-1, keepdims=True)
    o_ref[...] = h * lax.rsqrt(ms + EPS) * gf_ref[...]


def _out_ffn(x2, od, osb, wo, g2, wu, wd, gf):
    rows = x2.shape[0]
    per_seq = od.shape[2] // TM
    const = lambda i: (0, 0)
    return pl.pallas_call(
        _out_ffn_kernel,
        grid=(rows // TM,),
        in_specs=[
            pl.BlockSpec((TM, D_MODEL), lambda i: (i, 0)),
            pl.BlockSpec((1, DIFF_WIDTH, TM), lambda i: (i // per_seq, 0, i % per_seq)),
            pl.BlockSpec((1, SB_WIDTH, TM), lambda i: (i // per_seq, 0, i % per_seq)),
            pl.BlockSpec((D_MODEL, D_MODEL), const),
            pl.BlockSpec((1, D_MODEL), const),
            pl.BlockSpec((D_MODEL, D_FF), const),
            pl.BlockSpec((D_FF, D_MODEL), const),
            pl.BlockSpec((1, D_MODEL), const),
        ],
        out_specs=pl.BlockSpec((TM, D_MODEL), lambda i: (i, 0)),
        out_shape=jax.ShapeDtypeStruct((rows, D_MODEL), F32),
        compiler_params=pltpu.CompilerParams(
            dimension_semantics=("arbitrary",), vmem_limit_bytes=VMEM_LIMIT),
        name="out_ffn",
    )(x2, od, osb, wo, g2, wu, wd, gf)


def kernel(x, norm1_g, w_in, lambda_q1, lambda_k1, lambda_q2, lambda_k2, diff_subln_g,
           sb_norm_g, w_out, norm2_g, w_up, w_down, final_norm_g):
    B, S, D = x.shape
    x2 = x.reshape(B * S, D)
    slopes2 = (np.exp2(-8.0 * np.arange(1, N_DIFF_HEADS + 1) / N_DIFF_HEADS) * LOG2E).astype(np.float32)

    q, klo, khi, vd, vs, wu, wd, wo = _norm_proj(x2, norm1_g[0][None, :], w_in[0], w_up[0],
                                                 w_down[0], w_out[0])
    q3, klo3, khi3 = (a.reshape(B, S, Q_WIDTH) for a in (q, klo, khi))

    o_diff = _diff_attention(q3, klo3, khi3, vd, slopes2, lambda_q1, lambda_k1, lambda_q2,
                             lambda_k2, diff_subln_g[0][:, None])
    g_pair = jnp.concatenate([sb_norm_g[0], sb_norm_g[0]])[:, None]
    o_sb = _stick_breaking(q3, klo3, khi3, vs, g_pair)

    out = _out_ffn(x2, o_diff, o_sb, wo, norm2_g[0][None, :], wu, wd, final_norm_g[None, :])
    return out.reshape(B, S, D)
```

```python
import math

import numpy as np
import jax
import jax.numpy as jnp
from jax import lax
from jax.experimental import pallas as pl
from jax.experimental.pallas import tpu as pltpu

D_MODEL = 1024
HEAD_DIM = 64
CHUNK = 64
N_DIFF_HEADS = 4
N_SB_HEADS = 8
N_SB_PAIRS = N_SB_HEADS // 2
DIFF_V_DIM = 2 * HEAD_DIM
DIFF_WIDTH = N_DIFF_HEADS * DIFF_V_DIM
SB_WIDTH = N_SB_HEADS * HEAD_DIM
Q_WIDTH = DIFF_WIDTH + SB_WIDTH
D_FF = 4 * D_MODEL
EPS = 1e-6
NEG_INF = -1e30
LAMBDA_INIT = 0.8 - 0.6 * math.exp(-0.3 * 0)
LOG2E = 1.4426950408889634

LANES = 128
BF16_ROWS = 16
TQ = 256
TKD = 256
TKS = 128
NSLAB = TQ // LANES
TM = 512
VD_ROWS = DIFF_V_DIM + BF16_ROWS
VMEM_LIMIT = 56 * 1024 * 1024

F32 = jnp.float32
BF16 = jnp.bfloat16


def _nt_dot(a, b):
    return lax.dot_general(a, b, (((1,), (1,)), ((), ())), preferred_element_type=F32)


def _norm_proj_kernel(x_ref, g_ref, win_ref, wu_ref, wd_ref, wo_ref,
                      oq_ref, oklo_ref, okhi_ref, ovd_ref, ovs_ref, owu_ref, owd_ref, owo_ref,
                      wq_ref, wk_ref, wvt_ref):
    @pl.when(pl.program_id(0) == 0)
    def _():
        d1, d2, d3 = DIFF_WIDTH, 2 * DIFF_WIDTH, 3 * DIFF_WIDTH
        s1, s2 = d3 + SB_WIDTH, d3 + 2 * SB_WIDTH
        wq_ref[:, :DIFF_WIDTH] = win_ref[:, :d1].astype(BF16)
        wq_ref[:, DIFF_WIDTH:] = win_ref[:, d3:s1].astype(BF16)
        wk_ref[:, :DIFF_WIDTH] = win_ref[:, d1:d2].astype(BF16)
        wk_ref[:, DIFF_WIDTH:] = win_ref[:, s1:s2].astype(BF16)
        wvt_ref[:DIFF_WIDTH, :] = win_ref[:, d2:d3].T.astype(BF16)
        wvt_ref[DIFF_WIDTH:, :] = win_ref[:, s2:].T.astype(BF16)

    owu_ref[...] = wu_ref[...].astype(BF16)
    owd_ref[...] = wd_ref[...].astype(BF16)
    owo_ref[...] = wo_ref[...].astype(BF16)

    x = x_ref[...]
    ms = jnp.mean(x * x, axis=-1, keepdims=True)
    n = (x * lax.rsqrt(ms + EPS) * g_ref[...]).astype(BF16)
    chunk = 512
    lane = lax.broadcasted_iota(jnp.int32, (1, chunk), 1) % LANES
    lo = jnp.where(lane < HEAD_DIM, 1.0, 0.0)
    hi = 1.0 - lo
    qscale = LOG2E / math.sqrt(HEAD_DIM)
    for c in range(Q_WIDTH // chunk):
        sl = slice(c * chunk, (c + 1) * chunk)
        oq_ref[:, sl] = (jnp.dot(n, wq_ref[:, sl], preferred_element_type=F32) * qscale).astype(BF16)
        kk = jnp.dot(n, wk_ref[:, sl], preferred_element_type=F32)
        oklo_ref[:, sl] = (kk * lo).astype(BF16)
        okhi_ref[:, sl] = (kk * hi).astype(BF16)
    vd = _nt_dot(wvt_ref[:DIFF_WIDTH, :], n)
    prow = lax.broadcasted_iota(jnp.int32, (BF16_ROWS, TM), 0)
    ones_rows = jnp.where(prow == 0, 1.0, 0.0).astype(BF16)
    for h in range(N_DIFF_HEADS):
        ovd_ref[h * VD_ROWS:h * VD_ROWS + DIFF_V_DIM, :] = (
            vd[h * DIFF_V_DIM:(h + 1) * DIFF_V_DIM].astype(BF16))
        ovd_ref[h * VD_ROWS + DIFF_V_DIM:(h + 1) * VD_ROWS, :] = ones_rows
    vs = _nt_dot(wvt_ref[DIFF_WIDTH:, :], n)
    row = lax.broadcasted_iota(jnp.int32, (SB_WIDTH, 1), 0) % LANES
    ra = jnp.where(row < HEAD_DIM, 1.0, 0.0)
    ovs_ref[:SB_WIDTH, :] = (vs * ra).astype(BF16)
    ovs_ref[SB_WIDTH:, :] = (vs * (1.0 - ra)).astype(BF16)


def _norm_proj(x2, g, w_in, w_up, w_down, w_out):
    rows = x2.shape[0]
    steps = rows // TM
    const = lambda i: (0, 0)
    row_blk = lambda w: pl.BlockSpec((TM, w), lambda i: (i, 0))
    col_blk = lambda r: pl.BlockSpec((r, TM), lambda i: (0, i))
    wu_blk = pl.BlockSpec((D_MODEL, D_FF // steps), lambda i: (0, i))
    wd_blk = pl.BlockSpec((D_FF // steps, D_MODEL), lambda i: (i, 0))
    wo_blk = pl.BlockSpec((D_MODEL // steps, D_MODEL), lambda i: (i, 0))
    return pl.pallas_call(
        _norm_proj_kernel,
        grid=(steps,),
        in_specs=[
            row_blk(D_MODEL),
            pl.BlockSpec((1, D_MODEL), const),
            pl.BlockSpec(w_in.shape, const),
            wu_blk, wd_blk, wo_blk,
        ],
        out_specs=[row_blk(Q_WIDTH), row_blk(Q_WIDTH), row_blk(Q_WIDTH),
                   col_blk(N_DIFF_HEADS * VD_ROWS), col_blk(2 * SB_WIDTH),
                   wu_blk, wd_blk, wo_blk],
        out_shape=[jax.ShapeDtypeStruct((rows, Q_WIDTH), BF16),
                   jax.ShapeDtypeStruct((rows, Q_WIDTH), BF16),
                   jax.ShapeDtypeStruct((rows, Q_WIDTH), BF16),
                   jax.ShapeDtypeStruct((N_DIFF_HEADS * VD_ROWS, rows), BF16),
                   jax.ShapeDtypeStruct((2 * SB_WIDTH, rows), BF16),
                   jax.ShapeDtypeStruct(w_up.shape, BF16),
                   jax.ShapeDtypeStruct(w_down.shape, BF16),
                   jax.ShapeDtypeStruct(w_out.shape, BF16)],
        scratch_shapes=[pltpu.VMEM((D_MODEL, Q_WIDTH), BF16),
                        pltpu.VMEM((D_MODEL, Q_WIDTH), BF16),
                        pltpu.VMEM((Q_WIDTH, D_MODEL), BF16)],
        compiler_params=pltpu.CompilerParams(
            dimension_semantics=("arbitrary",), vmem_limit_bytes=VMEM_LIMIT),
        name="norm_proj",
    )(x2, g, w_in, w_up, w_down, w_out)


def _stage_table(items, lags, fields, defaults):
    n = len(items)
    nsteps = n + max(lags)
    nsteps += nsteps % 2
    rows = [(s, f) for s in range(len(lags)) for f in fields[s]]
    tab = np.zeros((len(rows), nsteps), np.int32)
    for ri, (s, f) in enumerate(rows):
        for t in range(nsteps):
            idx = t - lags[s]
            if 0 <= idx < n:
                tab[ri, t] = items[idx][f]
            elif f in defaults:
                tab[ri, t] = defaults[f]
            else:
                tab[ri, t] = items[min(max(idx, 0), n - 1)][f]
    return tab, {(s, f): ri for ri, (s, f) in enumerate(rows)}


def _diff_schedule(seq):
    items = []
    for i in range(seq // TQ):
        t0 = i * TQ
        items.append(dict(q=t0, k=t0, kind=0, keep=0, dt=0, last=0))
        for j in range(i):
            items.append(dict(q=t0, k=j * TKD, kind=1, keep=1, dt=t0 - j * TKD, last=0))
        items[-1]["last"] = 1
    return _stage_table(items, lags=(0, 1, 2),
                        fields=(("q", "k"), ("kind", "keep", "dt"), ("k", "q", "last")),
                        defaults=dict(keep=1, last=0))


def _sb_schedule(seq):
    items = []
    nd = TQ // TKS
    for i in range(seq // TQ):
        t0 = i * TQ
        first = len(items)
        for d in reversed(range(nd)):
            items.append([t0, t0 + d * TKS, nd - 1 - d, 1, i, 0])
        for j in reversed(range(nd * i)):
            items.append([t0, j * TKS, nd, 1, i, 0])
        items[first][3] = 0
        for it in items[first:]:
            it[5] = len(items)
    return np.asarray(items, np.int32).T.copy()


def _make_diff_kernel(rix):
    heads = range(N_DIFF_HEADS)

    def kernel(tab_ref, slopes_ref, lq1_ref, lk1_ref, lq2_ref, lk2_ref, g_ref, bias_ref,
               q_ref, klo_ref, khi_ref, vd_ref, o_ref,
               s0_scr, s1_scr, p0_scr, p1_scr, al0_scr, al1_scr, m_scr, acc_scr):
        s_scr, p_scr, alpha_scr = (s0_scr, s1_scr), (p0_scr, p1_scr), (al0_scr, al1_scr)
        lam = (jnp.exp(jnp.sum(lq1_ref[...] * lk1_ref[...], axis=1, keepdims=True))
               - jnp.exp(jnp.sum(lq2_ref[...] * lk2_ref[...], axis=1, keepdims=True))
               + LAMBDA_INIT)

        for ref in s_scr + p_scr + alpha_scr:
            ref[...] = jnp.zeros(ref.shape, ref.dtype)
        m_scr[...] = jnp.full(m_scr.shape, NEG_INF, F32)
        acc_scr[...] = jnp.zeros(acc_scr.shape, F32)

        def step(t, cur):
            prev = 1 - cur
            qs = pl.multiple_of(tab_ref[rix[0, "q"], t], TQ)
            ks = pl.multiple_of(tab_ref[rix[0, "k"], t], TKD)
            for h in heads:
                cols = slice(h * LANES, (h + 1) * LANES)
                q = q_ref[0, pl.ds(qs, TQ), cols]
                kbd = jnp.concatenate([klo_ref[0, pl.ds(ks, TKD), cols],
                                       khi_ref[0, pl.ds(ks, TKD), cols]], axis=0)
                s = _nt_dot(kbd, q)
                for half in range(NSLAB):
                    s_scr[cur][h, half] = s[:, half * LANES:(half + 1) * LANES]

            ks = pl.multiple_of(tab_ref[rix[2, "k"], t], TKD)
            for h in heads:
                vt = vd_ref[h * VD_ROWS:(h + 1) * VD_ROWS, pl.ds(ks, TKD)]
                for m in range(2):
                    row = 2 * h + m
                    p = jnp.concatenate([p_scr[prev][row, half] for half in range(NSLAB)], axis=1)
                    pv = jnp.dot(vt, p, preferred_element_type=F32)
                    acc_scr[row] = acc_scr[row] * alpha_scr[prev][row:row + 1, :] + pv

            kind = tab_ref[rix[1, "kind"], t]
            keep = tab_ref[rix[1, "keep"], t].astype(F32)
            dt = tab_ref[rix[1, "dt"], t].astype(F32)
            for h in heads:
                shift = -slopes_ref[h] * dt
                for m in range(2):
                    row = 2 * h + m
                    for half in range(NSLAB):
                        ls = slice(half * LANES, (half + 1) * LANES)
                        u = (s_scr[prev][h, half, m * TKD:(m + 1) * TKD, :]
                             + bias_ref[h, kind, half])
                        m_old = m_scr[row:row + 1, ls] * keep + NEG_INF * (1.0 - keep)
                        m_new = jnp.maximum(m_old, jnp.max(u, axis=0, keepdims=True) + shift)
                        p_scr[cur][row, half] = jnp.exp2(u - (m_new - shift)).astype(BF16)
                        alpha_scr[cur][row:row + 1, ls] = jnp.exp2(m_old - m_new)
                        m_scr[row:row + 1, ls] = m_new

            @pl.when(tab_ref[rix[2, "last"], t] == 1)
            def _():
                qf = pl.multiple_of(tab_ref[rix[2, "q"], t], TQ)
                for h in heads:
                    a1 = acc_scr[2 * h]
                    a2 = acc_scr[2 * h + 1]
                    r1 = 1.0 / a1[DIFF_V_DIM:DIFF_V_DIM + 1]
                    r2 = lam / a2[DIFF_V_DIM:DIFF_V_DIM + 1]
                    o = a1[:DIFF_V_DIM] * r1 - a2[:DIFF_V_DIM] * r2
                    ms = jnp.mean(o * o, axis=0, keepdims=True)
                    o = o * lax.rsqrt(ms + EPS) * g_ref[...] * (1.0 - LAMBDA_INIT)
                    o_ref[0, h * LANES:(h + 1) * LANES, pl.ds(qf, TQ)] = o.astype(BF16)

        def two_steps(i, carry):
            step(2 * i, 0)
            step(2 * i + 1, 1)
            return carry

        lax.fori_loop(0, tab_ref.shape[1] // 2, two_steps, 0)

    return kernel


def _diff_attention(q3, klo3, khi3, vd, slopes2, lq1, lk1, lq2, lk2, subln_g_col):
    B, S, _ = q3.shape
    tab, rix = _diff_schedule(S)
    c = np.arange(TKD)[:, None]
    r = np.arange(TQ)[None, :]
    slopes = np.asarray(slopes2, np.float64)[:, None, None]
    diag = np.where((c // CHUNK) <= (r // CHUNK), -slopes * np.abs(r - c), NEG_INF)
    full = -slopes * (r - c)
    bias = np.stack([diag, full], axis=1).astype(np.float32)
    bias = bias.reshape(N_DIFF_HEADS, 2, TKD, NSLAB, LANES).transpose(0, 1, 3, 2, 4)
    vec = pl.BlockSpec((1, HEAD_DIM), lambda b, tab: (0, 0))
    seq_blk = pl.BlockSpec((1, S, DIFF_WIDTH), lambda b, tab: (b, 0, 0))
    nmap = 2 * N_DIFF_HEADS
    grid_spec = pltpu.PrefetchScalarGridSpec(
        num_scalar_prefetch=1,
        grid=(B,),
        in_specs=[
            pl.BlockSpec(memory_space=pltpu.SMEM),
            vec, vec, vec, vec,
            pl.BlockSpec((DIFF_V_DIM, 1), lambda b, tab: (0, 0)),
            pl.BlockSpec((N_DIFF_HEADS, 2, NSLAB, TKD, LANES), lambda b, tab: (0, 0, 0, 0, 0)),
            seq_blk, seq_blk, seq_blk,
            pl.BlockSpec((N_DIFF_HEADS * VD_ROWS, S), lambda b, tab: (0, b)),
        ],
        out_specs=pl.BlockSpec((1, DIFF_WIDTH, S), lambda b, tab: (b, 0, 0)),
        scratch_shapes=[
            pltpu.VMEM((N_DIFF_HEADS, NSLAB, 2 * TKD, LANES), F32),
            pltpu.VMEM((N_DIFF_HEADS, NSLAB, 2 * TKD, LANES), F32),
            pltpu.VMEM((nmap, NSLAB, TKD, LANES), BF16),
            pltpu.VMEM((nmap, NSLAB, TKD, LANES), BF16),
            pltpu.VMEM((nmap, TQ), F32),
            pltpu.VMEM((nmap, TQ), F32),
            pltpu.VMEM((nmap, TQ), F32),
            pltpu.VMEM((nmap, VD_ROWS, TQ), F32),
        ],
    )
    return pl.pallas_call(
        _make_diff_kernel(rix),
        grid_spec=grid_spec,
        out_shape=jax.ShapeDtypeStruct((B, DIFF_WIDTH, S), BF16),
        compiler_params=pltpu.CompilerParams(
            dimension_semantics=("arbitrary",), vmem_limit_bytes=VMEM_LIMIT),
        name="diff_attn",
    )(jnp.asarray(tab), slopes2, lq1, lk1, lq2, lk2, subln_g_col, jnp.asarray(bias),
      q3, klo3, khi3, vd)


SB_Q, SB_K, SB_MSEL, SB_KEEP, SB_BLK, SB_NEXT = range(6)
SB_DEAD = 160.0


def _stick_break_kernel(tab_ref, g_ref, mask_ref, stair_ref, q_ref, klo_ref, khi_ref, vs_ref, o_ref,
                        z_scr, sp_scr, a_scr, acc_scr, c_scr):
    n_items = tab_ref.shape[1]
    pairs = range(N_SB_PAIRS)
    su = lax.broadcasted_iota(jnp.int32, (2 * TKS, 2 * TKS), 0)
    ju = lax.broadcasted_iota(jnp.int32, (2 * TKS, 2 * TKS), 1)
    utbd = jnp.where((ju >= su) & ((ju // TKS) == (su // TKS)), 1.0, 0.0).astype(BF16)

    z_scr[...] = jnp.full(z_scr.shape, NEG_INF, F32)
    sp_scr[...] = jnp.zeros(sp_scr.shape, BF16)
    a_scr[...] = jnp.zeros(a_scr.shape, BF16)
    acc_scr[...] = jnp.zeros(acc_scr.shape, F32)
    c_scr[...] = jnp.zeros(c_scr.shape, F32)

    def step(carry):
        i1, i2, i3 = carry
        j1 = jnp.minimum(i1, n_items - 1)
        j2 = jnp.minimum(i2, n_items - 1)
        j3 = jnp.minimum(i3, n_items - 1)

        ks = pl.multiple_of(tab_ref[SB_K, j3], TKS)
        for p in pairs:
            rows = slice(p * LANES, (p + 1) * LANES)
            vbd = jnp.concatenate([vs_ref[rows, pl.ds(ks, TKS)],
                                   vs_ref[SB_WIDTH + p * LANES:SB_WIDTH + (p + 1) * LANES,
                                          pl.ds(ks, TKS)]], axis=1)
            acc_scr[p] = acc_scr[p] + jnp.dot(vbd, a_scr[p], preferred_element_type=F32)

        keep2 = tab_ref[SB_KEEP, j2].astype(F32)
        cmin = None
        for p in pairs:
            suf = jnp.dot(utbd, sp_scr[p], preferred_element_type=F32)
            ca = c_scr[2 * p:2 * p + 1, :] * keep2
            cb = c_scr[2 * p + 1:2 * p + 2, :] * keep2
            cpair = jnp.concatenate([jnp.broadcast_to(ca, (TKS, TQ)),
                                     jnp.broadcast_to(cb, (TKS, TQ))], axis=0)
            a_scr[p] = jnp.exp2((z_scr[p] - suf - cpair).astype(BF16))
            ca = ca + suf[0:1, :]
            cb = cb + suf[TKS:TKS + 1, :]
            c_scr[2 * p:2 * p + 1, :] = ca
            c_scr[2 * p + 1:2 * p + 2, :] = cb
            m = jnp.minimum(ca, cb)
            cmin = m if cmin is None else jnp.minimum(cmin, m)
        dead = jnp.min(cmin) >= SB_DEAD

        qs = pl.multiple_of(tab_ref[SB_Q, j1], TQ)
        ks = pl.multiple_of(tab_ref[SB_K, j1], TKS)
        qcode = mask_ref[tab_ref[SB_MSEL, j1]]
        kstair = stair_ref[...]
        for p in pairs:
            cols = slice(p * LANES, (p + 1) * LANES)
            q = jnp.concatenate([q_ref[0, pl.ds(qs, TQ), cols], qcode], axis=1)
            kbd = jnp.concatenate([klo_ref[0, pl.ds(ks, TKS), cols],
                                   khi_ref[0, pl.ds(ks, TKS), cols]], axis=0)
            kbd = jnp.concatenate([kbd, kstair], axis=1)
            z = _nt_dot(kbd, q)
            sp = jnp.maximum(jnp.log2(1.0 + jnp.exp2(jnp.minimum(z, 126.0))), z)
            z_scr[p] = z
            sp_scr[p] = sp.astype(BF16)

        blk3 = tab_ref[SB_BLK, j3]
        last3 = (i3 < n_items) & ((i2 >= n_items) | (tab_ref[SB_BLK, j2] != blk3))

        @pl.when(last3)
        def _():
            qf = pl.multiple_of(tab_ref[SB_Q, j3], TQ)
            for p in pairs:
                o = acc_scr[p]
                o2 = o * o
                msa = jnp.mean(o2[:HEAD_DIM], axis=0, keepdims=True)
                msb = jnp.mean(o2[HEAD_DIM:], axis=0, keepdims=True)
                ms = jnp.concatenate([jnp.broadcast_to(msa, (HEAD_DIM, TQ)),
                                      jnp.broadcast_to(msb, (HEAD_DIM, TQ))], axis=0)
                o = o * lax.rsqrt(ms + EPS) * g_ref[...]
                o_ref[0, p * LANES:(p + 1) * LANES, pl.ds(qf, TQ)] = o.astype(BF16)
                acc_scr[p] = jnp.zeros((LANES, TQ), F32)

        same_blk = (i2 < n_items) & (tab_ref[SB_BLK, j2] == tab_ref[SB_BLK, j1])
        nxt = jnp.where(dead & same_blk, tab_ref[SB_NEXT, j1], i1 + 1)
        nxt = jnp.where(i1 < n_items, jnp.minimum(nxt, n_items), n_items)
        return nxt, i1, i2

    def busy(carry):
        i1, i2, i3 = carry
        return (i1 < n_items) | (i2 < n_items) | (i3 < n_items)

    none = jnp.int32(n_items)
    lax.while_loop(busy, step, (jnp.int32(0), none, none))


def _stick_breaking(q3, klo3, khi3, vs, g_pair_col):
    B, S, _ = q3.shape
    tab = _sb_schedule(S)
    nd = TQ // TKS
    r = np.arange(TQ)[:, None]
    x = np.arange(TKS)[None, :]
    codes = []
    for d in reversed(range(nd)):
        rel = r - d * TKS
        codes.append(((np.maximum(rel, 0) == x) & (rel < TKS)).astype(np.float32))
    codes.append(np.zeros((TQ, TKS), np.float32))
    mask = jnp.asarray(np.stack(codes), BF16)
    c = (np.arange(2 * TKS) % TKS)[:, None]
    stair = jnp.asarray(np.where(c >= x, NEG_INF, 0.0), BF16)
    seq_blk = pl.BlockSpec((1, S, SB_WIDTH), lambda b, tab: (b, 0, 1))
    tile = (N_SB_PAIRS, 2 * TKS, TQ)
    grid_spec = pltpu.PrefetchScalarGridSpec(
        num_scalar_prefetch=1,
        grid=(B,),
        in_specs=[
            pl.BlockSpec((LANES, 1), lambda b, tab: (0, 0)),
            pl.BlockSpec((nd + 1, TQ, TKS), lambda b, tab: (0, 0, 0)),
            pl.BlockSpec((2 * TKS, TKS), lambda b, tab: (0, 0)),
            seq_blk, seq_blk, seq_blk,
            pl.BlockSpec((2 * SB_WIDTH, S), lambda b, tab: (0, b)),
        ],
        out_specs=pl.BlockSpec((1, SB_WIDTH, S), lambda b, tab: (b, 0, 0)),
        scratch_shapes=[
            pltpu.VMEM(tile, F32),
            pltpu.VMEM(tile, BF16),
            pltpu.VMEM(tile, BF16),
            pltpu.VMEM((N_SB_PAIRS, LANES, TQ), F32),
            pltpu.VMEM((2 * N_SB_PAIRS, TQ), F32),
        ],
    )
    return pl.pallas_call(
        _stick_break_kernel,
        grid_spec=grid_spec,
        out_shape=jax.ShapeDtypeStruct((B, SB_WIDTH, S), BF16),
        compiler_params=pltpu.CompilerParams(
            dimension_semantics=("arbitrary",), vmem_limit_bytes=VMEM_LIMIT),
        name="stick_break",
    )(jnp.asarray(tab), g_pair_col, mask, stair, q3, klo3, khi3, vs)


def _out_ffn_kernel(x_ref, od_ref, os_ref, wo_ref, g2_ref, wu_ref, wd_ref, gf_ref, o_ref):
    tn = (((0,), (0,)), ((), ()))
    h = (x_ref[...]
         + lax.dot_general(od_ref[0], wo_ref[:DIFF_WIDTH, :], tn, preferred_element_type=F32)
         + lax.dot_general(os_ref[0], wo_ref[DIFF_WIDTH:, :], tn, preferred_element_type=F32))
    ms = jnp.mean(h * h, axis=-1, keepdims=True)
    n2 = (h * lax.rsqrt(ms + EPS) * g2_ref[...]).astype(BF16)
    u = jnp.dot(n2, wu_ref[...], preferred_element_type=F32)
    u = jnp.square(jnp.maximum(u, 0.0)).astype(BF16)
    h = h + jnp.dot(u, wd_ref[...], preferred_element_type=F32)
    ms = jnp.mean(h * h, axis=-1, keepdims=True)
    o_ref[...] = h * lax.rsqrt(ms + EPS) * gf_ref[...]


def _out_ffn(x2, od, osb, wo, g2, wu, wd, gf):
    rows = x2.shape[0]
    per_seq = od.shape[2] // TM
    const = lambda i: (0, 0)
    return pl.pallas_call(
        _out_ffn_kernel,
        grid=(rows // TM,),
        in_specs=[
            pl.BlockSpec((TM, D_MODEL), lambda i: (i, 0)),
            pl.BlockSpec((1, DIFF_WIDTH, TM), lambda i: (i // per_seq, 0, i % per_seq)),
            pl.BlockSpec((1, SB_WIDTH, TM), lambda i: (i // per_seq, 0, i % per_seq)),
            pl.BlockSpec((D_MODEL, D_MODEL), const),
            pl.BlockSpec((1, D_MODEL), const),
            pl.BlockSpec((D_MODEL, D_FF), const),
            pl.BlockSpec((D_FF, D_MODEL), const),
            pl.BlockSpec((1, D_MODEL), const),
        ],
        out_specs=pl.BlockSpec((TM, D_MODEL), lambda i: (i, 0)),
        out_shape=jax.ShapeDtypeStruct((rows, D_MODEL), F32),
        compiler_params=pltpu.CompilerParams(
            dimension_semantics=("arbitrary",), vmem_limit_bytes=VMEM_LIMIT),
        name="out_ffn",
    )(x2, od, osb, wo, g2, wu, wd, gf)


def kernel(x, norm1_g, w_in, lambda_q1, lambda_k1, lambda_q2, lambda_k2, diff_subln_g,
           sb_norm_g, w_out, norm2_g, w_up, w_down, final_norm_g):
    B, S, D = x.shape
    x2 = x.reshape(B * S, D)
    slopes2 = (np.exp2(-8.0 * np.arange(1, N_DIFF_HEADS + 1) / N_DIFF_HEADS) * LOG2E).astype(np.float32)

    q, klo, khi, vd, vs, wu, wd, wo = _norm_proj(x2, norm1_g[0][None, :], w_in[0], w_up[0],
                                                 w_down[0], w_out[0])
    q3, klo3, khi3 = (a.reshape(B, S, Q_WIDTH) for a in (q, klo, khi))

    o_diff = _diff_attention(q3, klo3, khi3, vd, slopes2, lambda_q1, lambda_k1, lambda_q2,
                             lambda_k2, diff_subln_g[0][:, None])
    g_pair = jnp.concatenate([sb_norm_g[0], sb_norm_g[0]])[:, None]
    o_sb = _stick_breaking(q3, klo3, khi3, vs, g_pair)

    out = _out_ffn(x2, o_diff, o_sb, wo, norm2_g[0][None, :], wu, wd, final_norm_g[None, :])
    return out.reshape(B, S, D)
```

```python
import math

import numpy as np
import jax
import jax.numpy as jnp
from jax import lax
from jax.experimental import pallas as pl
from jax.experimental.pallas import tpu as pltpu

D_MODEL = 1024
HEAD_DIM = 64
CHUNK = 64
N_DIFF_HEADS = 4
N_SB_HEADS = 8
N_SB_PAIRS = N_SB_HEADS // 2
DIFF_V_DIM = 2 * HEAD_DIM
DIFF_WIDTH = N_DIFF_HEADS * DIFF_V_DIM
SB_WIDTH = N_SB_HEADS * HEAD_DIM
Q_WIDTH = DIFF_WIDTH + SB_WIDTH
D_FF = 4 * D_MODEL
EPS = 1e-6
NEG_INF = -1e30
LAMBDA_INIT = 0.8 - 0.6 * math.exp(-0.3 * 0)
LOG2E = 1.4426950408889634

LANES = 128
BF16_ROWS = 16
TQ = 256
TKD = 256
TKS = 128
NSLAB = TQ // LANES
TM = 512
TM_PROJ = 1024
VD_ROWS = DIFF_V_DIM + BF16_ROWS
VMEM_LIMIT = 56 * 1024 * 1024

F32 = jnp.float32
BF16 = jnp.bfloat16


def _nt_dot(a, b):
    return lax.dot_general(a, b, (((1,), (1,)), ((), ())), preferred_element_type=F32)


def _norm_proj_kernel(x_ref, g_ref, win_ref, wu_ref, wd_ref, wo_ref,
                      oq_ref, oklo_ref, okhi_ref, ovd_ref, ovs_ref, owu_ref, owd_ref, owo_ref,
                      wq_ref, wk_ref, wvt_ref):
    @pl.when(pl.program_id(0) == 0)
    def _():
        d1, d2, d3 = DIFF_WIDTH, 2 * DIFF_WIDTH, 3 * DIFF_WIDTH
        s1, s2 = d3 + SB_WIDTH, d3 + 2 * SB_WIDTH
        wq_ref[:, :DIFF_WIDTH] = win_ref[:, :d1].astype(BF16)
        wq_ref[:, DIFF_WIDTH:] = win_ref[:, d3:s1].astype(BF16)
        wk_ref[:, :DIFF_WIDTH] = win_ref[:, d1:d2].astype(BF16)
        wk_ref[:, DIFF_WIDTH:] = win_ref[:, s1:s2].astype(BF16)
        wvt_ref[:DIFF_WIDTH, :] = win_ref[:, d2:d3].T.astype(BF16)
        wvt_ref[DIFF_WIDTH:, :] = win_ref[:, s2:].T.astype(BF16)

    owu_ref[...] = wu_ref[...].astype(BF16)
    owd_ref[...] = wd_ref[...].astype(BF16)
    owo_ref[...] = wo_ref[...].astype(BF16)

    x = x_ref[...]
    ms = jnp.mean(x * x, axis=-1, keepdims=True)
    n = (x * lax.rsqrt(ms + EPS) * g_ref[...]).astype(BF16)
    chunk = 512
    lane = lax.broadcasted_iota(jnp.int32, (1, chunk), 1) % LANES
    lo = jnp.where(lane < HEAD_DIM, 1.0, 0.0)
    hi = 1.0 - lo
    qscale = LOG2E / math.sqrt(HEAD_DIM)
    for c in range(Q_WIDTH // chunk):
        sl = slice(c * chunk, (c + 1) * chunk)
        oq_ref[:, sl] = (jnp.dot(n, wq_ref[:, sl], preferred_element_type=F32) * qscale).astype(BF16)
        kk = jnp.dot(n, wk_ref[:, sl], preferred_element_type=F32)
        oklo_ref[:, sl] = (kk * lo).astype(BF16)
        okhi_ref[:, sl] = (kk * hi).astype(BF16)
    vd = _nt_dot(wvt_ref[:DIFF_WIDTH, :], n)
    prow = lax.broadcasted_iota(jnp.int32, (BF16_ROWS, TM_PROJ), 0)
    ones_rows = jnp.where(prow == 0, 1.0, 0.0).astype(BF16)
    for h in range(N_DIFF_HEADS):
        ovd_ref[h * VD_ROWS:h * VD_ROWS + DIFF_V_DIM, :] = (
            vd[h * DIFF_V_DIM:(h + 1) * DIFF_V_DIM].astype(BF16))
        ovd_ref[h * VD_ROWS + DIFF_V_DIM:(h + 1) * VD_ROWS, :] = ones_rows
    vs = _nt_dot(wvt_ref[DIFF_WIDTH:, :], n)
    row = lax.broadcasted_iota(jnp.int32, (SB_WIDTH, 1), 0) % LANES
    ra = jnp.where(row < HEAD_DIM, 1.0, 0.0)
    ovs_ref[:SB_WIDTH, :] = (vs * ra).astype(BF16)
    ovs_ref[SB_WIDTH:, :] = (vs * (1.0 - ra)).astype(BF16)


def _norm_proj(x2, g, w_in, w_up, w_down, w_out):
    rows = x2.shape[0]
    steps = rows // TM_PROJ
    const = lambda i: (0, 0)
    row_blk = lambda w: pl.BlockSpec((TM_PROJ, w), lambda i: (i, 0))
    col_blk = lambda r: pl.BlockSpec((r, TM_PROJ), lambda i: (0, i))
    wu_blk = pl.BlockSpec((D_MODEL, D_FF // steps), lambda i: (0, i))
    wd_blk = pl.BlockSpec((D_FF // steps, D_MODEL), lambda i: (i, 0))
    wo_blk = pl.BlockSpec((D_MODEL // steps, D_MODEL), lambda i: (i, 0))
    return pl.pallas_call(
        _norm_proj_kernel,
        grid=(steps,),
        in_specs=[
            row_blk(D_MODEL),
            pl.BlockSpec((1, D_MODEL), const),
            pl.BlockSpec(w_in.shape, const),
            wu_blk, wd_blk, wo_blk,
        ],
        out_specs=[row_blk(Q_WIDTH), row_blk(Q_WIDTH), row_blk(Q_WIDTH),
                   col_blk(N_DIFF_HEADS * VD_ROWS), col_blk(2 * SB_WIDTH),
                   wu_blk, wd_blk, wo_blk],
        out_shape=[jax.ShapeDtypeStruct((rows, Q_WIDTH), BF16),
                   jax.ShapeDtypeStruct((rows, Q_WIDTH), BF16),
                   jax.ShapeDtypeStruct((rows, Q_WIDTH), BF16),
                   jax.ShapeDtypeStruct((N_DIFF_HEADS * VD_ROWS, rows), BF16),
                   jax.ShapeDtypeStruct((2 * SB_WIDTH, rows), BF16),
                   jax.ShapeDtypeStruct(w_up.shape, BF16),
                   jax.ShapeDtypeStruct(w_down.shape, BF16),
                   jax.ShapeDtypeStruct(w_out.shape, BF16)],
        scratch_shapes=[pltpu.VMEM((D_MODEL, Q_WIDTH), BF16),
                        pltpu.VMEM((D_MODEL, Q_WIDTH), BF16),
                        pltpu.VMEM((Q_WIDTH, D_MODEL), BF16)],
        compiler_params=pltpu.CompilerParams(
            dimension_semantics=("arbitrary",), vmem_limit_bytes=VMEM_LIMIT),
        name="norm_proj",
    )(x2, g, w_in, w_up, w_down, w_out)


def _stage_table(items, lags, fields, defaults):
    n = len(items)
    nsteps = n + max(lags)
    nsteps += nsteps % 2
    rows = [(s, f) for s in range(len(lags)) for f in fields[s]]
    tab = np.zeros((len(rows), nsteps), np.int32)
    for ri, (s, f) in enumerate(rows):
        for t in range(nsteps):
            idx = t - lags[s]
            if 0 <= idx < n:
                tab[ri, t] = items[idx][f]
            elif f in defaults:
                tab[ri, t] = defaults[f]
            else:
                tab[ri, t] = items[min(max(idx, 0), n - 1)][f]
    return tab, {(s, f): ri for ri, (s, f) in enumerate(rows)}


def _diff_schedule(seq):
    items = []
    for i in range(seq // TQ):
        t0 = i * TQ
        items.append(dict(q=t0, k=t0, kind=0, keep=0, dt=0, last=0))
        for j in range(i):
            items.append(dict(q=t0, k=j * TKD, kind=1, keep=1, dt=t0 - j * TKD, last=0))
        items[-1]["last"] = 1
    return _stage_table(items, lags=(0, 1, 2),
                        fields=(("q", "k"), ("kind", "keep", "dt"), ("k", "q", "last")),
                        defaults=dict(keep=1, last=0))


def _sb_schedule(seq):
    items = []
    nd = TQ // TKS
    for i in range(seq // TQ):
        t0 = i * TQ
        first = len(items)
        for d in reversed(range(nd)):
            items.append([t0, t0 + d * TKS, nd - 1 - d, 1, i, 0])
        for j in reversed(range(nd * i)):
            items.append([t0, j * TKS, nd, 1, i, 0])
        items[first][3] = 0
        for it in items[first:]:
            it[5] = len(items)
    return np.asarray(items, np.int32).T.copy()


def _make_diff_kernel(rix):
    heads = range(N_DIFF_HEADS)

    def kernel(tab_ref, slopes_ref, lq1_ref, lk1_ref, lq2_ref, lk2_ref, g_ref, bias_ref,
               q_ref, klo_ref, khi_ref, vd_ref, o_ref,
               s0_scr, s1_scr, p0_scr, p1_scr, al0_scr, al1_scr, m_scr, acc_scr):
        s_scr, p_scr, alpha_scr = (s0_scr, s1_scr), (p0_scr, p1_scr), (al0_scr, al1_scr)
        lam = (jnp.exp(jnp.sum(lq1_ref[...] * lk1_ref[...], axis=1, keepdims=True))
               - jnp.exp(jnp.sum(lq2_ref[...] * lk2_ref[...], axis=1, keepdims=True))
               + LAMBDA_INIT)

        @pl.when(pl.program_id(0) == 0)
        def _():
            for ref in s_scr + p_scr + alpha_scr:
                ref[...] = jnp.zeros(ref.shape, ref.dtype)
            m_scr[...] = jnp.full(m_scr.shape, NEG_INF, F32)
            acc_scr[...] = jnp.zeros(acc_scr.shape, F32)

        def step(t, cur):
            prev = 1 - cur
            qs = pl.multiple_of(tab_ref[rix[0, "q"], t], TQ)
            ks = pl.multiple_of(tab_ref[rix[0, "k"], t], TKD)
            for h in heads:
                cols = slice(h * LANES, (h + 1) * LANES)
                q = q_ref[0, pl.ds(qs, TQ), cols]
                kbd = jnp.concatenate([klo_ref[0, pl.ds(ks, TKD), cols],
                                       khi_ref[0, pl.ds(ks, TKD), cols]], axis=0)
                s = _nt_dot(kbd, q)
                for half in range(NSLAB):
                    s_scr[cur][h, half] = s[:, half * LANES:(half + 1) * LANES]

            ks = pl.multiple_of(tab_ref[rix[2, "k"], t], TKD)
            for h in heads:
                vt = vd_ref[h * VD_ROWS:(h + 1) * VD_ROWS, pl.ds(ks, TKD)]
                for m in range(2):
                    row = 2 * h + m
                    p = jnp.concatenate([p_scr[prev][row, half] for half in range(NSLAB)], axis=1)
                    pv = jnp.dot(vt, p, preferred_element_type=F32)
                    acc_scr[row] = acc_scr[row] * alpha_scr[prev][row:row + 1, :] + pv

            kind = tab_ref[rix[1, "kind"], t]
            keep = tab_ref[rix[1, "keep"], t].astype(F32)
            dt = tab_ref[rix[1, "dt"], t].astype(F32)
            for h in heads:
                shift = -slopes_ref[h] * dt
                for m in range(2):
                    row = 2 * h + m
                    for half in range(NSLAB):
                        ls = slice(half * LANES, (half + 1) * LANES)
                        u = (s_scr[prev][h, half, m * TKD:(m + 1) * TKD, :]
                             + bias_ref[h, kind, half])
                        m_old = m_scr[row:row + 1, ls] * keep + NEG_INF * (1.0 - keep)
                        m_new = jnp.maximum(m_old, jnp.max(u, axis=0, keepdims=True) + shift)
                        p_scr[cur][row, half] = jnp.exp2(u - (m_new - shift)).astype(BF16)
                        alpha_scr[cur][row:row + 1, ls] = jnp.exp2(m_old - m_new)
                        m_scr[row:row + 1, ls] = m_new

            @pl.when(tab_ref[rix[2, "last"], t] == 1)
            def _():
                qf = pl.multiple_of(tab_ref[rix[2, "q"], t], TQ)
                gain = g_ref[...] * (1.0 - LAMBDA_INIT)
                for h in heads:
                    a1 = acc_scr[2 * h]
                    a2 = acc_scr[2 * h + 1]
                    r1 = 1.0 / a1[DIFF_V_DIM:DIFF_V_DIM + 1]
                    r2 = lam / a2[DIFF_V_DIM:DIFF_V_DIM + 1]
                    o = a1[:DIFF_V_DIM] * r1 - a2[:DIFF_V_DIM] * r2
                    ms = jnp.mean(o * o, axis=0, keepdims=True)
                    o = o * lax.rsqrt(ms + EPS) * gain
                    o_ref[0, h * LANES:(h + 1) * LANES, pl.ds(qf, TQ)] = o.astype(BF16)

        def two_steps(i, carry):
            step(2 * i, 0)
            step(2 * i + 1, 1)
            return carry

        lax.fori_loop(0, tab_ref.shape[1] // 2, two_steps, 0)

    return kernel


def _diff_attention(q3, klo3, khi3, vd, slopes2, lq1, lk1, lq2, lk2, subln_g_col):
    B, S, _ = q3.shape
    tab, rix = _diff_schedule(S)
    c = np.arange(TKD)[:, None]
    r = np.arange(TQ)[None, :]
    slopes = np.asarray(slopes2, np.float64)[:, None, None]
    diag = np.where((c // CHUNK) <= (r // CHUNK), -slopes * np.abs(r - c), NEG_INF)
    full = -slopes * (r - c)
    bias = np.stack([diag, full], axis=1).astype(np.float32)
    bias = bias.reshape(N_DIFF_HEADS, 2, TKD, NSLAB, LANES).transpose(0, 1, 3, 2, 4)
    vec = pl.BlockSpec((1, HEAD_DIM), lambda b, tab: (0, 0))
    seq_blk = pl.BlockSpec((1, S, DIFF_WIDTH), lambda b, tab: (b, 0, 0))
    nmap = 2 * N_DIFF_HEADS
    grid_spec = pltpu.PrefetchScalarGridSpec(
        num_scalar_prefetch=1,
        grid=(B,),
        in_specs=[
            pl.BlockSpec(memory_space=pltpu.SMEM),
            vec, vec, vec, vec,
            pl.BlockSpec((DIFF_V_DIM, 1), lambda b, tab: (0, 0)),
            pl.BlockSpec((N_DIFF_HEADS, 2, NSLAB, TKD, LANES), lambda b, tab: (0, 0, 0, 0, 0)),
            seq_blk, seq_blk, seq_blk,
            pl.BlockSpec((N_DIFF_HEADS * VD_ROWS, S), lambda b, tab: (0, b)),
        ],
        out_specs=pl.BlockSpec((1, DIFF_WIDTH, S), lambda b, tab: (b, 0, 0)),
        scratch_shapes=[
            pltpu.VMEM((N_DIFF_HEADS, NSLAB, 2 * TKD, LANES), F32),
            pltpu.VMEM((N_DIFF_HEADS, NSLAB, 2 * TKD, LANES), F32),
            pltpu.VMEM((nmap, NSLAB, TKD, LANES), BF16),
            pltpu.VMEM((nmap, NSLAB, TKD, LANES), BF16),
            pltpu.VMEM((nmap, TQ), F32),
            pltpu.VMEM((nmap, TQ), F32),
            pltpu.VMEM((nmap, TQ), F32),
            pltpu.VMEM((nmap, VD_ROWS, TQ), F32),
        ],
    )
    return pl.pallas_call(
        _make_diff_kernel(rix),
        grid_spec=grid_spec,
        out_shape=jax.ShapeDtypeStruct((B, DIFF_WIDTH, S), BF16),
        compiler_params=pltpu.CompilerParams(
            dimension_semantics=("arbitrary",), vmem_limit_bytes=VMEM_LIMIT),
        name="diff_attn",
    )(jnp.asarray(tab), slopes2, lq1, lk1, lq2, lk2, subln_g_col, jnp.asarray(bias),
      q3, klo3, khi3, vd)


SB_Q, SB_K, SB_MSEL, SB_KEEP, SB_BLK, SB_NEXT = range(6)
SB_DEAD = 160.0


def _stick_break_kernel(tab_ref, g_ref, mask_ref, stair_ref, q_ref, klo_ref, khi_ref, vs_ref, o_ref,
                        z_scr, sp_scr, a_scr, acc_scr, c_scr):
    n_items = tab_ref.shape[1]
    pairs = range(N_SB_PAIRS)
    su = lax.broadcasted_iota(jnp.int32, (2 * TKS, 2 * TKS), 0)
    ju = lax.broadcasted_iota(jnp.int32, (2 * TKS, 2 * TKS), 1)
    utbd = jnp.where((ju >= su) & ((ju // TKS) == (su // TKS)), 1.0, 0.0).astype(BF16)

    z_scr[...] = jnp.full(z_scr.shape, NEG_INF, F32)
    sp_scr[...] = jnp.zeros(sp_scr.shape, BF16)
    a_scr[...] = jnp.zeros(a_scr.shape, BF16)
    acc_scr[...] = jnp.zeros(acc_scr.shape, F32)
    c_scr[...] = jnp.zeros(c_scr.shape, F32)

    def step(carry):
        i1, i2, i3 = carry
        j1 = jnp.minimum(i1, n_items - 1)
        j2 = jnp.minimum(i2, n_items - 1)
        j3 = jnp.minimum(i3, n_items - 1)

        ks = pl.multiple_of(tab_ref[SB_K, j3], TKS)
        for p in pairs:
            rows = slice(p * LANES, (p + 1) * LANES)
            vbd = jnp.concatenate([vs_ref[rows, pl.ds(ks, TKS)],
                                   vs_ref[SB_WIDTH + p * LANES:SB_WIDTH + (p + 1) * LANES,
                                          pl.ds(ks, TKS)]], axis=1)
            acc_scr[p] = acc_scr[p] + jnp.dot(vbd, a_scr[p], preferred_element_type=F32)

        keep2 = tab_ref[SB_KEEP, j2].astype(F32)
        cmin = None
        for p in pairs:
            suf = jnp.dot(utbd, sp_scr[p], preferred_element_type=F32)
            ca = c_scr[2 * p:2 * p + 1, :] * keep2
            cb = c_scr[2 * p + 1:2 * p + 2, :] * keep2
            cpair = jnp.concatenate([jnp.broadcast_to(ca, (TKS, TQ)),
                                     jnp.broadcast_to(cb, (TKS, TQ))], axis=0)
            a_scr[p] = jnp.exp2((z_scr[p] - suf - cpair).astype(BF16))
            ca = ca + suf[0:1, :]
            cb = cb + suf[TKS:TKS + 1, :]
            c_scr[2 * p:2 * p + 1, :] = ca
            c_scr[2 * p + 1:2 * p + 2, :] = cb
            m = jnp.minimum(ca, cb)
            cmin = m if cmin is None else jnp.minimum(cmin, m)
        dead = jnp.min(cmin) >= SB_DEAD

        qs = pl.multiple_of(tab_ref[SB_Q, j1], TQ)
        ks = pl.multiple_of(tab_ref[SB_K, j1], TKS)
        qcode = mask_ref[tab_ref[SB_MSEL, j1]]
        kstair = stair_ref[...]
        for p in pairs:
            cols = slice(p * LANES, (p + 1) * LANES)
            q = jnp.concatenate([q_ref[0, pl.ds(qs, TQ), cols], qcode], axis=1)
            kbd = jnp.concatenate([klo_ref[0, pl.ds(ks, TKS), cols],
                                   khi_ref[0, pl.ds(ks, TKS), cols]], axis=0)
            kbd = jnp.concatenate([kbd, kstair], axis=1)
            z = _nt_dot(kbd, q)
            sp = jnp.maximum(jnp.log2(1.0 + jnp.exp2(jnp.minimum(z, 126.0))), z)
            z_scr[p] = z
            sp_scr[p] = sp.astype(BF16)

        blk3 = tab_ref[SB_BLK, j3]
        last3 = (i3 < n_items) & ((i2 >= n_items) | (tab_ref[SB_BLK, j2] != blk3))

        @pl.when(last3)
        def _():
            qf = pl.multiple_of(tab_ref[SB_Q, j3], TQ)
            for p in pairs:
                o = acc_scr[p]
                o2 = o * o
                msa = jnp.mean(o2[:HEAD_DIM], axis=0, keepdims=True)
                msb = jnp.mean(o2[HEAD_DIM:], axis=0, keepdims=True)
                ms = jnp.concatenate([jnp.broadcast_to(msa, (HEAD_DIM, TQ)),
                                      jnp.broadcast_to(msb, (HEAD_DIM, TQ))], axis=0)
                o = o * lax.rsqrt(ms + EPS) * g_ref[...]
                o_ref[0, p * LANES:(p + 1) * LANES, pl.ds(qf, TQ)] = o.astype(BF16)
                acc_scr[p] = jnp.zeros((LANES, TQ), F32)

        same_blk = (i2 < n_items) & (tab_ref[SB_BLK, j2] == tab_ref[SB_BLK, j1])
        nxt = jnp.where(dead & same_blk, tab_ref[SB_NEXT, j1], i1 + 1)
        nxt = jnp.where(i1 < n_items, jnp.minimum(nxt, n_items), n_items)
        return nxt, i1, i2

    def busy(carry):
        i1, i2, i3 = carry
        return (i1 < n_items) | (i2 < n_items) | (i3 < n_items)

    none = jnp.int32(n_items)
    lax.while_loop(busy, step, (jnp.int32(0), none, none))


def _stick_breaking(q3, klo3, khi3, vs, g_pair_col):
    B, S, _ = q3.shape
    tab = _sb_schedule(S)
    nd = TQ // TKS
    r = np.arange(TQ)[:, None]
    x = np.arange(TKS)[None, :]
    codes = []
    for d in reversed(range(nd)):
        rel = r - d * TKS
        codes.append(((np.maximum(rel, 0) == x) & (rel < TKS)).astype(np.float32))
    codes.append(np.zeros((TQ, TKS), np.float32))
    mask = jnp.asarray(np.stack(codes), BF16)
    c = (np.arange(2 * TKS) % TKS)[:, None]
    stair = jnp.asarray(np.where(c >= x, NEG_INF, 0.0), BF16)
    seq_blk = pl.BlockSpec((1, S, SB_WIDTH), lambda b, tab: (b, 0, 1))
    tile = (N_SB_PAIRS, 2 * TKS, TQ)
    grid_spec = pltpu.PrefetchScalarGridSpec(
        num_scalar_prefetch=1,
        grid=(B,),
        in_specs=[
            pl.BlockSpec((LANES, 1), lambda b, tab: (0, 0)),
            pl.BlockSpec((nd + 1, TQ, TKS), lambda b, tab: (0, 0, 0)),
            pl.BlockSpec((2 * TKS, TKS), lambda b, tab: (0, 0)),
            seq_blk, seq_blk, seq_blk,
            pl.BlockSpec((2 * SB_WIDTH, S), lambda b, tab: (0, b)),
        ],
        out_specs=pl.BlockSpec((1, SB_WIDTH, S), lambda b, tab: (b, 0, 0)),
        scratch_shapes=[
            pltpu.VMEM(tile, F32),
            pltpu.VMEM(tile, BF16),
            pltpu.VMEM(tile, BF16),
            pltpu.VMEM((N_SB_PAIRS, LANES, TQ), F32),
            pltpu.VMEM((2 * N_SB_PAIRS, TQ), F32),
        ],
    )
    return pl.pallas_call(
        _stick_break_kernel,
        grid_spec=grid_spec,
        out_shape=jax.ShapeDtypeStruct((B, SB_WIDTH, S), BF16),
        compiler_params=pltpu.CompilerParams(
            dimension_semantics=("arbitrary",), vmem_limit_bytes=VMEM_LIMIT),
        name="stick_break",
    )(jnp.asarray(tab), g_pair_col, mask, stair, q3, klo3, khi3, vs)


def _out_ffn_kernel(x_ref, od_ref, os_ref, wo_ref, g2_ref, wu_ref, wd_ref, gf_ref, o_ref):
    tn = (((0,), (0,)), ((), ()))
    h = (x_ref[...]
         + lax.dot_general(od_ref[0], wo_ref[:DIFF_WIDTH, :], tn, preferred_element_type=F32)
         + lax.dot_general(os_ref[0], wo_ref[DIFF_WIDTH:, :], tn, preferred_element_type=F32))
    ms = jnp.mean(h * h, axis=-1, keepdims=True)
    n2 = (h * lax.rsqrt(ms + EPS) * g2_ref[...]).astype(BF16)
    u = jnp.dot(n2, wu_ref[...], preferred_element_type=F32)
    u = jnp.square(jnp.maximum(u, 0.0)).astype(BF16)
    h = h + jnp.dot(u, wd_ref[...], preferred_element_type=F32)
    ms = jnp.mean(h * h, axis=-1, keepdims=True)
    o_ref[...] = h * lax.rsqrt(ms + EPS) * gf_ref[...]


def _out_ffn(x2, od, osb, wo, g2, wu, wd, gf):
    rows = x2.shape[0]
    per_seq = od.shape[2] // TM
    const = lambda i: (0, 0)
    return pl.pallas_call(
        _out_ffn_kernel,
        grid=(rows // TM,),
        in_specs=[
            pl.BlockSpec((TM, D_MODEL), lambda i: (i, 0)),
            pl.BlockSpec((1, DIFF_WIDTH, TM), lambda i: (i // per_seq, 0, i % per_seq)),
            pl.BlockSpec((1, SB_WIDTH, TM), lambda i: (i // per_seq, 0, i % per_seq)),
            pl.BlockSpec((D_MODEL, D_MODEL), const),
            pl.BlockSpec((1, D_MODEL), const),
            pl.BlockSpec((D_MODEL, D_FF), const),
            pl.BlockSpec((D_FF, D_MODEL), const),
            pl.BlockSpec((1, D_MODEL), const),
        ],
        out_specs=pl.BlockSpec((TM, D_MODEL), lambda i: (i, 0)),
        out_shape=jax.ShapeDtypeStruct((rows, D_MODEL), F32),
        compiler_params=pltpu.CompilerParams(
            dimension_semantics=("arbitrary",), vmem_limit_bytes=VMEM_LIMIT),
        name="out_ffn",
    )(x2, od, osb, wo, g2, wu, wd, gf)


def kernel(x, norm1_g, w_in, lambda_q1, lambda_k1, lambda_q2, lambda_k2, diff_subln_g,
           sb_norm_g, w_out, norm2_g, w_up, w_down, final_norm_g):
    B, S, D = x.shape
    x2 = x.reshape(B * S, D)
    slopes2 = (np.exp2(-8.0 * np.arange(1, N_DIFF_HEADS + 1) / N_DIFF_HEADS) * LOG2E).astype(np.float32)

    q, klo, khi, vd, vs, wu, wd, wo = _norm_proj(x2, norm1_g[0][None, :], w_in[0], w_up[0],
                                                 w_down[0], w_out[0])
    q3, klo3, khi3 = (a.reshape(B, S, Q_WIDTH) for a in (q, klo, khi))

    o_diff = _diff_attention(q3, klo3, khi3, vd, slopes2, lambda_q1, lambda_k1, lambda_q2,
                             lambda_k2, diff_subln_g[0][:, None])
    g_pair = jnp.concatenate([sb_norm_g[0], sb_norm_g[0]])[:, None]
    o_sb = _stick_breaking(q3, klo3, khi3, vs, g_pair)

    out = _out_ffn(x2, o_diff, o_sb, wo, norm2_g[0][None, :], wu, wd, final_norm_g[None, :])
    return out.reshape(B, S, D)
```

```python
import math

import numpy as np
import jax
import jax.numpy as jnp
from jax import lax
from jax.experimental import pallas as pl
from jax.experimental.pallas import tpu as pltpu

D_MODEL = 1024
HEAD_DIM = 64
CHUNK = 64
N_DIFF_HEADS = 4
N_SB_HEADS = 8
N_SB_PAIRS = N_SB_HEADS // 2
DIFF_V_DIM = 2 * HEAD_DIM
DIFF_WIDTH = N_DIFF_HEADS * DIFF_V_DIM
SB_WIDTH = N_SB_HEADS * HEAD_DIM
Q_WIDTH = DIFF_WIDTH + SB_WIDTH
D_FF = 4 * D_MODEL
EPS = 1e-6
NEG_INF = -1e30
LAMBDA_INIT = 0.8 - 0.6 * math.exp(-0.3 * 0)
LOG2E = 1.4426950408889634

LANES = 128
BF16_ROWS = 16
TQ = 256
TKD = 256
TKS = 128
NSLAB = TQ // LANES
TM = 512
TM_PROJ = 1024
VD_ROWS = DIFF_V_DIM + BF16_ROWS
VMEM_LIMIT = 56 * 1024 * 1024

F32 = jnp.float32
BF16 = jnp.bfloat16


def _nt_dot(a, b):
    return lax.dot_general(a, b, (((1,), (1,)), ((), ())), preferred_element_type=F32)


def _norm_proj_kernel(x_ref, g_ref, win_ref, wu_ref, wd_ref, wo_ref,
                      oq_ref, oqt_ref, oklo_ref, okhi_ref, ovd_ref, ovs_ref, owu_ref, owd_ref,
                      owo_ref, wq_ref, wqt_ref, wk_ref, wvt_ref):
    @pl.when(pl.program_id(0) == 0)
    def _():
        d1, d2, d3 = DIFF_WIDTH, 2 * DIFF_WIDTH, 3 * DIFF_WIDTH
        s1, s2 = d3 + SB_WIDTH, d3 + 2 * SB_WIDTH
        wq_ref[...] = win_ref[:, :d1].astype(BF16)
        wqt_ref[...] = win_ref[:, d3:s1].T.astype(BF16)
        wk_ref[:, :DIFF_WIDTH] = win_ref[:, d1:d2].astype(BF16)
        wk_ref[:, DIFF_WIDTH:] = win_ref[:, s1:s2].astype(BF16)
        wvt_ref[:DIFF_WIDTH, :] = win_ref[:, d2:d3].T.astype(BF16)
        wvt_ref[DIFF_WIDTH:, :] = win_ref[:, s2:].T.astype(BF16)

    owu_ref[...] = wu_ref[...].astype(BF16)
    owd_ref[...] = wd_ref[...].astype(BF16)
    owo_ref[...] = wo_ref[...].astype(BF16)

    x = x_ref[...]
    ms = jnp.mean(x * x, axis=-1, keepdims=True)
    n = (x * lax.rsqrt(ms + EPS) * g_ref[...]).astype(BF16)
    chunk = 512
    lane = lax.broadcasted_iota(jnp.int32, (1, chunk), 1) % LANES
    lo = jnp.where(lane < HEAD_DIM, 1.0, 0.0)
    hi = 1.0 - lo
    qscale = LOG2E / math.sqrt(HEAD_DIM)
    oq_ref[...] = (jnp.dot(n, wq_ref[...], preferred_element_type=F32) * qscale).astype(BF16)
    oqt_ref[...] = (_nt_dot(wqt_ref[...], n) * qscale).astype(BF16)
    for c in range(Q_WIDTH // chunk):
        sl = slice(c * chunk, (c + 1) * chunk)
        kk = jnp.dot(n, wk_ref[:, sl], preferred_element_type=F32)
        oklo_ref[:, sl] = (kk * lo).astype(BF16)
        okhi_ref[:, sl] = (kk * hi).astype(BF16)
    vd = _nt_dot(wvt_ref[:DIFF_WIDTH, :], n)
    prow = lax.broadcasted_iota(jnp.int32, (BF16_ROWS, TM_PROJ), 0)
    ones_rows = jnp.where(prow == 0, 1.0, 0.0).astype(BF16)
    for h in range(N_DIFF_HEADS):
        ovd_ref[h * VD_ROWS:h * VD_ROWS + DIFF_V_DIM, :] = (
            vd[h * DIFF_V_DIM:(h + 1) * DIFF_V_DIM].astype(BF16))
        ovd_ref[h * VD_ROWS + DIFF_V_DIM:(h + 1) * VD_ROWS, :] = ones_rows
    vs = _nt_dot(wvt_ref[DIFF_WIDTH:, :], n)
    row = lax.broadcasted_iota(jnp.int32, (SB_WIDTH, 1), 0) % LANES
    ra = jnp.where(row < HEAD_DIM, 1.0, 0.0)
    ovs_ref[:SB_WIDTH, :] = (vs * ra).astype(BF16)
    ovs_ref[SB_WIDTH:, :] = (vs * (1.0 - ra)).astype(BF16)


def _norm_proj(x2, g, w_in, w_up, w_down, w_out):
    rows = x2.shape[0]
    steps = rows // TM_PROJ
    const = lambda i: (0, 0)
    row_blk = lambda w: pl.BlockSpec((TM_PROJ, w), lambda i: (i, 0))
    col_blk = lambda r: pl.BlockSpec((r, TM_PROJ), lambda i: (0, i))
    wu_blk = pl.BlockSpec((D_MODEL, D_FF // steps), lambda i: (0, i))
    wd_blk = pl.BlockSpec((D_FF // steps, D_MODEL), lambda i: (i, 0))
    wo_blk = pl.BlockSpec((D_MODEL // steps, D_MODEL), lambda i: (i, 0))
    return pl.pallas_call(
        _norm_proj_kernel,
        grid=(steps,),
        in_specs=[
            row_blk(D_MODEL),
            pl.BlockSpec((1, D_MODEL), const),
            pl.BlockSpec(w_in.shape, const),
            wu_blk, wd_blk, wo_blk,
        ],
        out_specs=[row_blk(DIFF_WIDTH), col_blk(SB_WIDTH), row_blk(Q_WIDTH), row_blk(Q_WIDTH),
                   col_blk(N_DIFF_HEADS * VD_ROWS), col_blk(2 * SB_WIDTH),
                   wu_blk, wd_blk, wo_blk],
        out_shape=[jax.ShapeDtypeStruct((rows, DIFF_WIDTH), BF16),
                   jax.ShapeDtypeStruct((SB_WIDTH, rows), BF16),
                   jax.ShapeDtypeStruct((rows, Q_WIDTH), BF16),
                   jax.ShapeDtypeStruct((rows, Q_WIDTH), BF16),
                   jax.ShapeDtypeStruct((N_DIFF_HEADS * VD_ROWS, rows), BF16),
                   jax.ShapeDtypeStruct((2 * SB_WIDTH, rows), BF16),
                   jax.ShapeDtypeStruct(w_up.shape, BF16),
                   jax.ShapeDtypeStruct(w_down.shape, BF16),
                   jax.ShapeDtypeStruct(w_out.shape, BF16)],
        scratch_shapes=[pltpu.VMEM((D_MODEL, DIFF_WIDTH), BF16),
                        pltpu.VMEM((SB_WIDTH, D_MODEL), BF16),
                        pltpu.VMEM((D_MODEL, Q_WIDTH), BF16),
                        pltpu.VMEM((Q_WIDTH, D_MODEL), BF16)],
        compiler_params=pltpu.CompilerParams(
            dimension_semantics=("arbitrary",), vmem_limit_bytes=VMEM_LIMIT),
        name="norm_proj",
    )(x2, g, w_in, w_up, w_down, w_out)


def _stage_table(items, lags, fields, defaults):
    n = len(items)
    nsteps = n + max(lags)
    nsteps += nsteps % 2
    rows = [(s, f) for s in range(len(lags)) for f in fields[s]]
    tab = np.zeros((len(rows), nsteps), np.int32)
    for ri, (s, f) in enumerate(rows):
        for t in range(nsteps):
            idx = t - lags[s]
            if 0 <= idx < n:
                tab[ri, t] = items[idx][f]
            elif f in defaults:
                tab[ri, t] = defaults[f]
            else:
                tab[ri, t] = items[min(max(idx, 0), n - 1)][f]
    return tab, {(s, f): ri for ri, (s, f) in enumerate(rows)}


def _diff_schedule(seq):
    items = []
    for i in range(seq // TQ):
        t0 = i * TQ
        items.append(dict(q=t0, k=t0, kind=0, keep=0, dt=0, last=0))
        for j in range(i):
            items.append(dict(q=t0, k=j * TKD, kind=1, keep=1, dt=t0 - j * TKD, last=0))
        items[-1]["last"] = 1
    return _stage_table(items, lags=(0, 1, 2),
                        fields=(("q", "k"), ("kind", "keep", "dt"), ("k", "q", "last")),
                        defaults=dict(keep=1, last=0))


def _sb_schedule(seq):
    items = []
    nd = TQ // TKS
    for i in range(seq // TQ):
        t0 = i * TQ
        first = len(items)
        for d in reversed(range(nd)):
            items.append([t0, t0 + d * TKS, nd - 1 - d, 1, i, 0])
        for j in reversed(range(nd * i)):
            items.append([t0, j * TKS, nd, 1, i, 0])
        items[first][3] = 0
        for it in items[first:]:
            it[5] = len(items)
    return np.asarray(items, np.int32).T.copy()


def _make_diff_kernel(rix):
    heads = range(N_DIFF_HEADS)

    def kernel(tab_ref, slopes_ref, lq1_ref, lk1_ref, lq2_ref, lk2_ref, g_ref, bias_ref,
               q_ref, klo_ref, khi_ref, vd_ref, o_ref,
               s0_scr, s1_scr, p0_scr, p1_scr, al0_scr, al1_scr, m_scr, acc_scr):
        s_scr, p_scr, alpha_scr = (s0_scr, s1_scr), (p0_scr, p1_scr), (al0_scr, al1_scr)
        lam = (jnp.exp(jnp.sum(lq1_ref[...] * lk1_ref[...], axis=1, keepdims=True))
               - jnp.exp(jnp.sum(lq2_ref[...] * lk2_ref[...], axis=1, keepdims=True))
               + LAMBDA_INIT)

        @pl.when(pl.program_id(0) == 0)
        def _():
            for ref in s_scr + p_scr + alpha_scr:
                ref[...] = jnp.zeros(ref.shape, ref.dtype)
            m_scr[...] = jnp.full(m_scr.shape, NEG_INF, F32)
            acc_scr[...] = jnp.zeros(acc_scr.shape, F32)

        def step(t, cur):
            prev = 1 - cur
            qs = pl.multiple_of(tab_ref[rix[0, "q"], t], TQ)
            ks = pl.multiple_of(tab_ref[rix[0, "k"], t], TKD)
            for h in heads:
                cols = slice(h * LANES, (h + 1) * LANES)
                q = q_ref[0, pl.ds(qs, TQ), cols]
                kbd = jnp.concatenate([klo_ref[0, pl.ds(ks, TKD), cols],
                                       khi_ref[0, pl.ds(ks, TKD), cols]], axis=0)
                s = _nt_dot(kbd, q)
                for half in range(NSLAB):
                    s_scr[cur][h, half] = s[:, half * LANES:(half + 1) * LANES]

            ks = pl.multiple_of(tab_ref[rix[2, "k"], t], TKD)
            for h in heads:
                vt = vd_ref[h * VD_ROWS:(h + 1) * VD_ROWS, pl.ds(ks, TKD)]
                for m in range(2):
                    row = 2 * h + m
                    p = jnp.concatenate([p_scr[prev][row, half] for half in range(NSLAB)], axis=1)
                    pv = jnp.dot(vt, p, preferred_element_type=F32)
                    acc_scr[row] = acc_scr[row] * alpha_scr[prev][row:row + 1, :] + pv

            kind = tab_ref[rix[1, "kind"], t]
            keep = tab_ref[rix[1, "keep"], t].astype(F32)
            dt = tab_ref[rix[1, "dt"], t].astype(F32)
            for h in heads:
                shift = -slopes_ref[h] * dt
                for m in range(2):
                    row = 2 * h + m
                    for half in range(NSLAB):
                        ls = slice(half * LANES, (half + 1) * LANES)
                        u = (s_scr[prev][h, half, m * TKD:(m + 1) * TKD, :]
                             + bias_ref[h, kind, half])
                        m_old = m_scr[row:row + 1, ls] * keep + NEG_INF * (1.0 - keep)
                        m_new = jnp.maximum(m_old, jnp.max(u, axis=0, keepdims=True) + shift)
                        p_scr[cur][row, half] = jnp.exp2(u - (m_new - shift)).astype(BF16)
                        alpha_scr[cur][row:row + 1, ls] = jnp.exp2(m_old - m_new)
                        m_scr[row:row + 1, ls] = m_new

            @pl.when(tab_ref[rix[2, "last"], t] == 1)
            def _():
                qf = pl.multiple_of(tab_ref[rix[2, "q"], t], TQ)
                gain = g_ref[...] * (1.0 - LAMBDA_INIT)
                for h in heads:
                    a1 = acc_scr[2 * h]
                    a2 = acc_scr[2 * h + 1]
                    r1 = 1.0 / a1[DIFF_V_DIM:DIFF_V_DIM + 1]
                    r2 = lam / a2[DIFF_V_DIM:DIFF_V_DIM + 1]
                    o = a1[:DIFF_V_DIM] * r1 - a2[:DIFF_V_DIM] * r2
                    ms = jnp.mean(o * o, axis=0, keepdims=True)
                    o = o * lax.rsqrt(ms + EPS) * gain
                    o_ref[0, h * LANES:(h + 1) * LANES, pl.ds(qf, TQ)] = o.astype(BF16)

        def two_steps(i, carry):
            step(2 * i, 0)
            step(2 * i + 1, 1)
            return carry

        lax.fori_loop(0, tab_ref.shape[1] // 2, two_steps, 0)

    return kernel


def _diff_attention(q3, klo3, khi3, vd, slopes2, lq1, lk1, lq2, lk2, subln_g_col):
    B, S, _ = q3.shape
    tab, rix = _diff_schedule(S)
    c = np.arange(TKD)[:, None]
    r = np.arange(TQ)[None, :]
    slopes = np.asarray(slopes2, np.float64)[:, None, None]
    diag = np.where((c // CHUNK) <= (r // CHUNK), -slopes * np.abs(r - c), NEG_INF)
    full = -slopes * (r - c)
    bias = np.stack([diag, full], axis=1).astype(np.float32)
    bias = bias.reshape(N_DIFF_HEADS, 2, TKD, NSLAB, LANES).transpose(0, 1, 3, 2, 4)
    vec = pl.BlockSpec((1, HEAD_DIM), lambda b, tab: (0, 0))
    seq_blk = pl.BlockSpec((1, S, DIFF_WIDTH), lambda b, tab: (b, 0, 0))
    nmap = 2 * N_DIFF_HEADS
    grid_spec = pltpu.PrefetchScalarGridSpec(
        num_scalar_prefetch=1,
        grid=(B,),
        in_specs=[
            pl.BlockSpec(memory_space=pltpu.SMEM),
            vec, vec, vec, vec,
            pl.BlockSpec((DIFF_V_DIM, 1), lambda b, tab: (0, 0)),
            pl.BlockSpec((N_DIFF_HEADS, 2, NSLAB, TKD, LANES), lambda b, tab: (0, 0, 0, 0, 0)),
            seq_blk, seq_blk, seq_blk,
            pl.BlockSpec((N_DIFF_HEADS * VD_ROWS, S), lambda b, tab: (0, b)),
        ],
        out_specs=pl.BlockSpec((1, DIFF_WIDTH, S), lambda b, tab: (b, 0, 0)),
        scratch_shapes=[
            pltpu.VMEM((N_DIFF_HEADS, NSLAB, 2 * TKD, LANES), F32),
            pltpu.VMEM((N_DIFF_HEADS, NSLAB, 2 * TKD, LANES), F32),
            pltpu.VMEM((nmap, NSLAB, TKD, LANES), BF16),
            pltpu.VMEM((nmap, NSLAB, TKD, LANES), BF16),
            pltpu.VMEM((nmap, TQ), F32),
            pltpu.VMEM((nmap, TQ), F32),
            pltpu.VMEM((nmap, TQ), F32),
            pltpu.VMEM((nmap, VD_ROWS, TQ), F32),
        ],
    )
    return pl.pallas_call(
        _make_diff_kernel(rix),
        grid_spec=grid_spec,
        out_shape=jax.ShapeDtypeStruct((B, DIFF_WIDTH, S), BF16),
        compiler_params=pltpu.CompilerParams(
            dimension_semantics=("arbitrary",), vmem_limit_bytes=VMEM_LIMIT),
        name="diff_attn",
    )(jnp.asarray(tab), slopes2, lq1, lk1, lq2, lk2, subln_g_col, jnp.asarray(bias),
      q3, klo3, khi3, vd)


SB_Q, SB_K, SB_MSEL, SB_KEEP, SB_BLK, SB_NEXT = range(6)
SB_DEAD = 160.0


def _stick_break_kernel(tab_ref, g_ref, mask_ref, stair_ref, qt_ref, klo_ref, khi_ref, vs_ref, o_ref,
                        z_scr, sp_scr, a_scr, acc_scr, c_scr):
    n_items = tab_ref.shape[1]
    pairs = range(N_SB_PAIRS)
    su = lax.broadcasted_iota(jnp.int32, (2 * TKS, 2 * TKS), 0)
    ju = lax.broadcasted_iota(jnp.int32, (2 * TKS, 2 * TKS), 1)
    utbd = jnp.where((ju >= su) & ((ju // TKS) == (su // TKS)), 1.0, 0.0).astype(BF16)

    z_scr[...] = jnp.full(z_scr.shape, NEG_INF, F32)
    sp_scr[...] = jnp.zeros(sp_scr.shape, BF16)
    a_scr[...] = jnp.zeros(a_scr.shape, BF16)
    acc_scr[...] = jnp.zeros(acc_scr.shape, F32)
    c_scr[...] = jnp.zeros(c_scr.shape, F32)

    def step(carry):
        i1, i2, i3 = carry
        j1 = jnp.minimum(i1, n_items - 1)
        j2 = jnp.minimum(i2, n_items - 1)
        j3 = jnp.minimum(i3, n_items - 1)

        ks = pl.multiple_of(tab_ref[SB_K, j3], TKS)
        for p in pairs:
            rows = slice(p * LANES, (p + 1) * LANES)
            vbd = jnp.concatenate([vs_ref[rows, pl.ds(ks, TKS)],
                                   vs_ref[SB_WIDTH + p * LANES:SB_WIDTH + (p + 1) * LANES,
                                          pl.ds(ks, TKS)]], axis=1)
            acc_scr[p] = acc_scr[p] + jnp.dot(vbd, a_scr[p], preferred_element_type=F32)

        keep2 = tab_ref[SB_KEEP, j2].astype(F32)
        cmin = None
        for p in pairs:
            suf = jnp.dot(utbd, sp_scr[p], preferred_element_type=F32)
            ca = c_scr[2 * p:2 * p + 1, :] * keep2
            cb = c_scr[2 * p + 1:2 * p + 2, :] * keep2
            cpair = jnp.concatenate([jnp.broadcast_to(ca, (TKS, TQ)),
                                     jnp.broadcast_to(cb, (TKS, TQ))], axis=0)
            a_scr[p] = jnp.exp2((z_scr[p] - suf - cpair).astype(BF16))
            ca = ca + suf[0:1, :]
            cb = cb + suf[TKS:TKS + 1, :]
            c_scr[2 * p:2 * p + 1, :] = ca
            c_scr[2 * p + 1:2 * p + 2, :] = cb
            m = jnp.minimum(ca, cb)
            cmin = m if cmin is None else jnp.minimum(cmin, m)
        dead = jnp.min(cmin) >= SB_DEAD

        qs = pl.multiple_of(tab_ref[SB_Q, j1], TQ)
        ks = pl.multiple_of(tab_ref[SB_K, j1], TKS)
        qcode = mask_ref[tab_ref[SB_MSEL, j1]]
        kstair = stair_ref[...]
        for p in pairs:
            cols = slice(p * LANES, (p + 1) * LANES)
            qt = jnp.concatenate([qt_ref[cols, pl.ds(qs, TQ)], qcode], axis=0)
            kbd = jnp.concatenate([klo_ref[0, pl.ds(ks, TKS), cols],
                                   khi_ref[0, pl.ds(ks, TKS), cols]], axis=0)
            kbd = jnp.concatenate([kbd, kstair], axis=1)
            z = jnp.dot(kbd, qt, preferred_element_type=F32)
            sp = jnp.maximum(jnp.log2(1.0 + jnp.exp2(jnp.minimum(z, 126.0))), z)
            z_scr[p] = z
            sp_scr[p] = sp.astype(BF16)

        blk3 = tab_ref[SB_BLK, j3]
        last3 = (i3 < n_items) & ((i2 >= n_items) | (tab_ref[SB_BLK, j2] != blk3))

        @pl.when(last3)
        def _():
            qf = pl.multiple_of(tab_ref[SB_Q, j3], TQ)
            for p in pairs:
                o = acc_scr[p]
                o2 = o * o
                msa = jnp.mean(o2[:HEAD_DIM], axis=0, keepdims=True)
                msb = jnp.mean(o2[HEAD_DIM:], axis=0, keepdims=True)
                ms = jnp.concatenate([jnp.broadcast_to(msa, (HEAD_DIM, TQ)),
                                      jnp.broadcast_to(msb, (HEAD_DIM, TQ))], axis=0)
                o = o * lax.rsqrt(ms + EPS) * g_ref[...]
                o_ref[0, p * LANES:(p + 1) * LANES, pl.ds(qf, TQ)] = o.astype(BF16)
                acc_scr[p] = jnp.zeros((LANES, TQ), F32)

        same_blk = (i2 < n_items) & (tab_ref[SB_BLK, j2] == tab_ref[SB_BLK, j1])
        nxt = jnp.where(dead & same_blk, tab_ref[SB_NEXT, j1], i1 + 1)
        nxt = jnp.where(i1 < n_items, jnp.minimum(nxt, n_items), n_items)
        return nxt, i1, i2

    def busy(carry):
        i1, i2, i3 = carry
        return (i1 < n_items) | (i2 < n_items) | (i3 < n_items)

    none = jnp.int32(n_items)
    lax.while_loop(busy, step, (jnp.int32(0), none, none))


def _stick_breaking(qt, klo3, khi3, vs, g_pair_col):
    B, S, _ = klo3.shape
    tab = _sb_schedule(S)
    nd = TQ // TKS
    r = np.arange(TQ)[:, None]
    x = np.arange(TKS)[None, :]
    codes = []
    for d in reversed(range(nd)):
        rel = r - d * TKS
        codes.append(((np.maximum(rel, 0) == x) & (rel < TKS)).astype(np.float32))
    codes.append(np.zeros((TQ, TKS), np.float32))
    mask = jnp.asarray(np.stack(codes).transpose(0, 2, 1), BF16)
    c = (np.arange(2 * TKS) % TKS)[:, None]
    stair = jnp.asarray(np.where(c >= x, NEG_INF, 0.0), BF16)
    seq_blk = pl.BlockSpec((1, S, SB_WIDTH), lambda b, tab: (b, 0, 1))
    tile = (N_SB_PAIRS, 2 * TKS, TQ)
    grid_spec = pltpu.PrefetchScalarGridSpec(
        num_scalar_prefetch=1,
        grid=(B,),
        in_specs=[
            pl.BlockSpec((LANES, 1), lambda b, tab: (0, 0)),
            pl.BlockSpec((nd + 1, TKS, TQ), lambda b, tab: (0, 0, 0)),
            pl.BlockSpec((2 * TKS, TKS), lambda b, tab: (0, 0)),
            pl.BlockSpec((SB_WIDTH, S), lambda b, tab: (0, b)),
            seq_blk, seq_blk,
            pl.BlockSpec((2 * SB_WIDTH, S), lambda b, tab: (0, b)),
        ],
        out_specs=pl.BlockSpec((1, SB_WIDTH, S), lambda b, tab: (b, 0, 0)),
        scratch_shapes=[
            pltpu.VMEM(tile, F32),
            pltpu.VMEM(tile, BF16),
            pltpu.VMEM(tile, BF16),
            pltpu.VMEM((N_SB_PAIRS, LANES, TQ), F32),
            pltpu.VMEM((2 * N_SB_PAIRS, TQ), F32),
        ],
    )
    return pl.pallas_call(
        _stick_break_kernel,
        grid_spec=grid_spec,
        out_shape=jax.ShapeDtypeStruct((B, SB_WIDTH, S), BF16),
        compiler_params=pltpu.CompilerParams(
            dimension_semantics=("arbitrary",), vmem_limit_bytes=VMEM_LIMIT),
        name="stick_break",
    )(jnp.asarray(tab), g_pair_col, mask, stair, qt, klo3, khi3, vs)


def _out_ffn_kernel(x_ref, od_ref, os_ref, wo_ref, g2_ref, wu_ref, wd_ref, gf_ref, o_ref):
    tn = (((0,), (0,)), ((), ()))
    h = (x_ref[...]
         + lax.dot_general(od_ref[0], wo_ref[:DIFF_WIDTH, :], tn, preferred_element_type=F32)
         + lax.dot_general(os_ref[0], wo_ref[DIFF_WIDTH:, :], tn, preferred_element_type=F32))
    ms = jnp.mean(h * h, axis=-1, keepdims=True)
    n2 = (h * lax.rsqrt(ms + EPS) * g2_ref[...]).astype(BF16)
    u = jnp.dot(n2, wu_ref[...], preferred_element_type=F32)
    u = jnp.square(jnp.maximum(u, 0.0)).astype(BF16)
    h = h + jnp.dot(u, wd_ref[...], preferred_element_type=F32)
    ms = jnp.mean(h * h, axis=-1, keepdims=True)
    o_ref[...] = h * lax.rsqrt(ms + EPS) * gf_ref[...]


def _out_ffn(x2, od, osb, wo, g2, wu, wd, gf):
    rows = x2.shape[0]
    per_seq = od.shape[2] // TM
    const = lambda i: (0, 0)
    return pl.pallas_call(
        _out_ffn_kernel,
        grid=(rows // TM,),
        in_specs=[
            pl.BlockSpec((TM, D_MODEL), lambda i: (i, 0)),
            pl.BlockSpec((1, DIFF_WIDTH, TM), lambda i: (i // per_seq, 0, i % per_seq)),
            pl.BlockSpec((1, SB_WIDTH, TM), lambda i: (i // per_seq, 0, i % per_seq)),
            pl.BlockSpec((D_MODEL, D_MODEL), const),
            pl.BlockSpec((1, D_MODEL), const),
            pl.BlockSpec((D_MODEL, D_FF), const),
            pl.BlockSpec((D_FF, D_MODEL), const),
            pl.BlockSpec((1, D_MODEL), const),
        ],
        out_specs=pl.BlockSpec((TM, D_MODEL), lambda i: (i, 0)),
        out_shape=jax.ShapeDtypeStruct((rows, D_MODEL), F32),
        compiler_params=pltpu.CompilerParams(
            dimension_semantics=("arbitrary",), vmem_limit_bytes=VMEM_LIMIT),
        name="out_ffn",
    )(x2, od, osb, wo, g2, wu, wd, gf)


def kernel(x, norm1_g, w_in, lambda_q1, lambda_k1, lambda_q2, lambda_k2, diff_subln_g,
           sb_norm_g, w_out, norm2_g, w_up, w_down, final_norm_g):
    B, S, D = x.shape
    x2 = x.reshape(B * S, D)
    slopes2 = (np.exp2(-8.0 * np.arange(1, N_DIFF_HEADS + 1) / N_DIFF_HEADS) * LOG2E).astype(np.float32)

    q, qt, klo, khi, vd, vs, wu, wd, wo = _norm_proj(x2, norm1_g[0][None, :], w_in[0], w_up[0],
                                                 w_down[0], w_out[0])
    q3 = q.reshape(B, S, DIFF_WIDTH)
    klo3, khi3 = (a.reshape(B, S, Q_WIDTH) for a in (klo, khi))

    o_diff = _diff_attention(q3, klo3, khi3, vd, slopes2, lambda_q1, lambda_k1, lambda_q2,
                             lambda_k2, diff_subln_g[0][:, None])
    g_pair = jnp.concatenate([sb_norm_g[0], sb_norm_g[0]])[:, None]
    o_sb = _stick_breaking(qt, klo3, khi3, vs, g_pair)

    out = _out_ffn(x2, o_diff, o_sb, wo, norm2_g[0][None, :], wu, wd, final_norm_g[None, :])
    return out.reshape(B, S, D)
```

```python
import math

import numpy as np
import jax
import jax.numpy as jnp
from jax import lax
from jax.experimental import pallas as pl
from jax.experimental.pallas import tpu as pltpu

D_MODEL = 1024
HEAD_DIM = 64
CHUNK = 64
N_DIFF_HEADS = 4
N_SB_HEADS = 8
N_SB_PAIRS = N_SB_HEADS // 2
DIFF_V_DIM = 2 * HEAD_DIM
DIFF_WIDTH = N_DIFF_HEADS * DIFF_V_DIM
SB_WIDTH = N_SB_HEADS * HEAD_DIM
Q_WIDTH = DIFF_WIDTH + SB_WIDTH
D_FF = 4 * D_MODEL
EPS = 1e-6
SB_FIELDS = 6
NEG_INF = -1e30
LAMBDA_INIT = 0.8 - 0.6 * math.exp(-0.3 * 0)
LOG2E = 1.4426950408889634

LANES = 128
BF16_ROWS = 16
TQ = 256
TKD = 256
TKS = 128
NSLAB = TQ // LANES
TM = 512
TM_PROJ = 1024
VD_ROWS = DIFF_V_DIM + BF16_ROWS
VMEM_LIMIT = 56 * 1024 * 1024

F32 = jnp.float32
BF16 = jnp.bfloat16


def _nt_dot(a, b):
    return lax.dot_general(a, b, (((1,), (1,)), ((), ())), preferred_element_type=F32)


def _norm_proj_kernel(x_ref, g_ref, win_ref, wu_ref, wd_ref, wo_ref,
                      oq_ref, oqt_ref, oklo_ref, okhi_ref, ovd_ref, ovs_ref, owu_ref, owd_ref,
                      owo_ref, wq_ref, wqt_ref, wk_ref, wvt_ref):
    @pl.when(pl.program_id(0) == 0)
    def _():
        d1, d2, d3 = DIFF_WIDTH, 2 * DIFF_WIDTH, 3 * DIFF_WIDTH
        s1, s2 = d3 + SB_WIDTH, d3 + 2 * SB_WIDTH
        wq_ref[...] = win_ref[:, :d1].astype(BF16)
        wqt_ref[...] = win_ref[:, d3:s1].T.astype(BF16)
        wk_ref[:, :DIFF_WIDTH] = win_ref[:, d1:d2].astype(BF16)
        wk_ref[:, DIFF_WIDTH:] = win_ref[:, s1:s2].astype(BF16)
        wvt_ref[:DIFF_WIDTH, :] = win_ref[:, d2:d3].T.astype(BF16)
        wvt_ref[DIFF_WIDTH:, :] = win_ref[:, s2:].T.astype(BF16)

    owu_ref[...] = wu_ref[...].astype(BF16)
    owd_ref[...] = wd_ref[...].astype(BF16)
    owo_ref[...] = wo_ref[...].astype(BF16)

    x = x_ref[...]
    ms = jnp.mean(x * x, axis=-1, keepdims=True)
    n = (x * lax.rsqrt(ms + EPS) * g_ref[...]).astype(BF16)
    chunk = 512
    lane = lax.broadcasted_iota(jnp.int32, (1, chunk), 1) % LANES
    lo = jnp.where(lane < HEAD_DIM, 1.0, 0.0)
    hi = 1.0 - lo
    qscale = LOG2E / math.sqrt(HEAD_DIM)
    oq_ref[...] = (jnp.dot(n, wq_ref[...], preferred_element_type=F32) * qscale).astype(BF16)
    oqt_ref[...] = (_nt_dot(wqt_ref[...], n) * qscale).astype(BF16)
    for c in range(Q_WIDTH // chunk):
        sl = slice(c * chunk, (c + 1) * chunk)
        kk = jnp.dot(n, wk_ref[:, sl], preferred_element_type=F32)
        oklo_ref[:, sl] = (kk * lo).astype(BF16)
        okhi_ref[:, sl] = (kk * hi).astype(BF16)
    vd = _nt_dot(wvt_ref[:DIFF_WIDTH, :], n)
    prow = lax.broadcasted_iota(jnp.int32, (BF16_ROWS, TM_PROJ), 0)
    ones_rows = jnp.where(prow == 0, 1.0, 0.0).astype(BF16)
    for h in range(N_DIFF_HEADS):
        ovd_ref[h * VD_ROWS:h * VD_ROWS + DIFF_V_DIM, :] = (
            vd[h * DIFF_V_DIM:(h + 1) * DIFF_V_DIM].astype(BF16))
        ovd_ref[h * VD_ROWS + DIFF_V_DIM:(h + 1) * VD_ROWS, :] = ones_rows
    vs = _nt_dot(wvt_ref[DIFF_WIDTH:, :], n)
    row = lax.broadcasted_iota(jnp.int32, (SB_WIDTH, 1), 0) % LANES
    ra = jnp.where(row < HEAD_DIM, 1.0, 0.0)
    ovs_ref[:SB_WIDTH, :] = (vs * ra).astype(BF16)
    ovs_ref[SB_WIDTH:, :] = (vs * (1.0 - ra)).astype(BF16)


def _norm_proj(x2, g, w_in, w_up, w_down, w_out):
    rows = x2.shape[0]
    steps = rows // TM_PROJ
    const = lambda i: (0, 0)
    row_blk = lambda w: pl.BlockSpec((TM_PROJ, w), lambda i: (i, 0))
    col_blk = lambda r: pl.BlockSpec((r, TM_PROJ), lambda i: (0, i))
    wu_blk = pl.BlockSpec((D_MODEL, D_FF // steps), lambda i: (0, i))
    wd_blk = pl.BlockSpec((D_FF // steps, D_MODEL), lambda i: (i, 0))
    wo_blk = pl.BlockSpec((D_MODEL // steps, D_MODEL), lambda i: (i, 0))
    return pl.pallas_call(
        _norm_proj_kernel,
        grid=(steps,),
        in_specs=[
            row_blk(D_MODEL),
            pl.BlockSpec((1, D_MODEL), const),
            pl.BlockSpec(w_in.shape, const),
            wu_blk, wd_blk, wo_blk,
        ],
        out_specs=[row_blk(DIFF_WIDTH), col_blk(SB_WIDTH), row_blk(Q_WIDTH), row_blk(Q_WIDTH),
                   col_blk(N_DIFF_HEADS * VD_ROWS), col_blk(2 * SB_WIDTH),
                   wu_blk, wd_blk, wo_blk],
        out_shape=[jax.ShapeDtypeStruct((rows, DIFF_WIDTH), BF16),
                   jax.ShapeDtypeStruct((SB_WIDTH, rows), BF16),
                   jax.ShapeDtypeStruct((rows, Q_WIDTH), BF16),
                   jax.ShapeDtypeStruct((rows, Q_WIDTH), BF16),
                   jax.ShapeDtypeStruct((N_DIFF_HEADS * VD_ROWS, rows), BF16),
                   jax.ShapeDtypeStruct((2 * SB_WIDTH, rows), BF16),
                   jax.ShapeDtypeStruct(w_up.shape, BF16),
                   jax.ShapeDtypeStruct(w_down.shape, BF16),
                   jax.ShapeDtypeStruct(w_out.shape, BF16)],
        scratch_shapes=[pltpu.VMEM((D_MODEL, DIFF_WIDTH), BF16),
                        pltpu.VMEM((SB_WIDTH, D_MODEL), BF16),
                        pltpu.VMEM((D_MODEL, Q_WIDTH), BF16),
                        pltpu.VMEM((Q_WIDTH, D_MODEL), BF16)],
        compiler_params=pltpu.CompilerParams(
            dimension_semantics=("arbitrary",), vmem_limit_bytes=VMEM_LIMIT),
        name="norm_proj",
    )(x2, g, w_in, w_up, w_down, w_out)


def _stage_table(items, lags, fields, defaults):
    n = len(items)
    nsteps = n + max(lags)
    nsteps += nsteps % 2
    rows = [(s, f) for s in range(len(lags)) for f in fields[s]]
    tab = np.zeros((len(rows), nsteps), np.int32)
    for ri, (s, f) in enumerate(rows):
        for t in range(nsteps):
            idx = t - lags[s]
            if 0 <= idx < n:
                tab[ri, t] = items[idx][f]
            elif f in defaults:
                tab[ri, t] = defaults[f]
            else:
                tab[ri, t] = items[min(max(idx, 0), n - 1)][f]
    return tab, {(s, f): ri for ri, (s, f) in enumerate(rows)}


def _diff_schedule(seq):
    items = []
    for i in range(seq // TQ):
        t0 = i * TQ
        items.append(dict(q=t0, k=t0, kind=0, keep=0, dt=0, last=0))
        for j in range(i):
            items.append(dict(q=t0, k=j * TKD, kind=1, keep=1, dt=t0 - j * TKD, last=0))
        items[-1]["last"] = 1
    return _stage_table(items, lags=(0, 1, 2),
                        fields=(("q", "k"), ("kind", "keep", "dt"), ("k", "q", "last")),
                        defaults=dict(keep=1, last=0))


def _sb_schedule(seq):
    nd = TQ // TKS
    streams = [[], []]
    for i in range(seq // TQ):
        items = streams[i % 2]
        t0 = i * TQ
        first = len(items)
        for d in reversed(range(nd)):
            items.append([t0, t0 + d * TKS, nd - 1 - d, 1, i, 0])
        for j in reversed(range(nd * i)):
            items.append([t0, j * TKS, nd, 1, i, 0])
        items[first][3] = 0
        for it in items[first:]:
            it[5] = len(items)
    counts = tuple(len(items) for items in streams)
    tab = np.zeros((2 * SB_FIELDS, max(counts)), np.int32)
    for st, items in enumerate(streams):
        arr = np.asarray(items, np.int32).T
        tab[st * SB_FIELDS:(st + 1) * SB_FIELDS, :counts[st]] = arr
        tab[st * SB_FIELDS:(st + 1) * SB_FIELDS, counts[st]:] = arr[:, -1:]
    return tab, counts


def _make_diff_kernel(rix):
    heads = range(N_DIFF_HEADS)

    def kernel(tab_ref, slopes_ref, lq1_ref, lk1_ref, lq2_ref, lk2_ref, g_ref, bias_ref,
               q_ref, klo_ref, khi_ref, vd_ref, o_ref,
               s0_scr, s1_scr, p0_scr, p1_scr, al0_scr, al1_scr, m_scr, acc_scr):
        s_scr, p_scr, alpha_scr = (s0_scr, s1_scr), (p0_scr, p1_scr), (al0_scr, al1_scr)
        lam = (jnp.exp(jnp.sum(lq1_ref[...] * lk1_ref[...], axis=1, keepdims=True))
               - jnp.exp(jnp.sum(lq2_ref[...] * lk2_ref[...], axis=1, keepdims=True))
               + LAMBDA_INIT)

        @pl.when(pl.program_id(0) == 0)
        def _():
            for ref in s_scr + p_scr + alpha_scr:
                ref[...] = jnp.zeros(ref.shape, ref.dtype)
            m_scr[...] = jnp.full(m_scr.shape, NEG_INF, F32)
            acc_scr[...] = jnp.zeros(acc_scr.shape, F32)

        def step(t, cur):
            prev = 1 - cur
            qs = pl.multiple_of(tab_ref[rix[0, "q"], t], TQ)
            ks = pl.multiple_of(tab_ref[rix[0, "k"], t], TKD)
            for h in heads:
                cols = slice(h * LANES, (h + 1) * LANES)
                q = q_ref[0, pl.ds(qs, TQ), cols]
                kbd = jnp.concatenate([klo_ref[0, pl.ds(ks, TKD), cols],
                                       khi_ref[0, pl.ds(ks, TKD), cols]], axis=0)
                s = _nt_dot(kbd, q)
                for half in range(NSLAB):
                    s_scr[cur][h, half] = s[:, half * LANES:(half + 1) * LANES]

            ks = pl.multiple_of(tab_ref[rix[2, "k"], t], TKD)
            for h in heads:
                vt = vd_ref[h * VD_ROWS:(h + 1) * VD_ROWS, pl.ds(ks, TKD)]
                for m in range(2):
                    row = 2 * h + m
                    p = jnp.concatenate([p_scr[prev][row, half] for half in range(NSLAB)], axis=1)
                    pv = jnp.dot(vt, p, preferred_element_type=F32)
                    acc_scr[row] = acc_scr[row] * alpha_scr[prev][row:row + 1, :] + pv

            kind = tab_ref[rix[1, "kind"], t]
            keep = tab_ref[rix[1, "keep"], t].astype(F32)
            dt = tab_ref[rix[1, "dt"], t].astype(F32)
            for h in heads:
                shift = -slopes_ref[h] * dt
                for m in range(2):
                    row = 2 * h + m
                    for half in range(NSLAB):
                        ls = slice(half * LANES, (half + 1) * LANES)
                        u = (s_scr[prev][h, half, m * TKD:(m + 1) * TKD, :]
                             + bias_ref[h, kind, half])
                        m_old = m_scr[row:row + 1, ls] * keep + NEG_INF * (1.0 - keep)
                        m_new = jnp.maximum(m_old, jnp.max(u, axis=0, keepdims=True) + shift)
                        p_scr[cur][row, half] = jnp.exp2(u - (m_new - shift)).astype(BF16)
                        alpha_scr[cur][row:row + 1, ls] = jnp.exp2(m_old - m_new)
                        m_scr[row:row + 1, ls] = m_new

            @pl.when(tab_ref[rix[2, "last"], t] == 1)
            def _():
                qf = pl.multiple_of(tab_ref[rix[2, "q"], t], TQ)
                gain = g_ref[...] * (1.0 - LAMBDA_INIT)
                for h in heads:
                    a1 = acc_scr[2 * h]
                    a2 = acc_scr[2 * h + 1]
                    r1 = 1.0 / a1[DIFF_V_DIM:DIFF_V_DIM + 1]
                    r2 = lam / a2[DIFF_V_DIM:DIFF_V_DIM + 1]
                    o = a1[:DIFF_V_DIM] * r1 - a2[:DIFF_V_DIM] * r2
                    ms = jnp.mean(o * o, axis=0, keepdims=True)
                    o = o * lax.rsqrt(ms + EPS) * gain
                    o_ref[0, h * LANES:(h + 1) * LANES, pl.ds(qf, TQ)] = o.astype(BF16)

        def two_steps(i, carry):
            step(2 * i, 0)
            step(2 * i + 1, 1)
            return carry

        lax.fori_loop(0, tab_ref.shape[1] // 2, two_steps, 0)

    return kernel


def _diff_attention(q3, klo3, khi3, vd, slopes2, lq1, lk1, lq2, lk2, subln_g_col):
    B, S, _ = q3.shape
    tab, rix = _diff_schedule(S)
    c = np.arange(TKD)[:, None]
    r = np.arange(TQ)[None, :]
    slopes = np.asarray(slopes2, np.float64)[:, None, None]
    diag = np.where((c // CHUNK) <= (r // CHUNK), -slopes * np.abs(r - c), NEG_INF)
    full = -slopes * (r - c)
    bias = np.stack([diag, full], axis=1).astype(np.float32)
    bias = bias.reshape(N_DIFF_HEADS, 2, TKD, NSLAB, LANES).transpose(0, 1, 3, 2, 4)
    vec = pl.BlockSpec((1, HEAD_DIM), lambda b, tab: (0, 0))
    seq_blk = pl.BlockSpec((1, S, DIFF_WIDTH), lambda b, tab: (b, 0, 0))
    nmap = 2 * N_DIFF_HEADS
    grid_spec = pltpu.PrefetchScalarGridSpec(
        num_scalar_prefetch=1,
        grid=(B,),
        in_specs=[
            pl.BlockSpec(memory_space=pltpu.SMEM),
            vec, vec, vec, vec,
            pl.BlockSpec((DIFF_V_DIM, 1), lambda b, tab: (0, 0)),
            pl.BlockSpec((N_DIFF_HEADS, 2, NSLAB, TKD, LANES), lambda b, tab: (0, 0, 0, 0, 0)),
            seq_blk, seq_blk, seq_blk,
            pl.BlockSpec((N_DIFF_HEADS * VD_ROWS, S), lambda b, tab: (0, b)),
        ],
        out_specs=pl.BlockSpec((1, DIFF_WIDTH, S), lambda b, tab: (b, 0, 0)),
        scratch_shapes=[
            pltpu.VMEM((N_DIFF_HEADS, NSLAB, 2 * TKD, LANES), F32),
            pltpu.VMEM((N_DIFF_HEADS, NSLAB, 2 * TKD, LANES), F32),
            pltpu.VMEM((nmap, NSLAB, TKD, LANES), BF16),
            pltpu.VMEM((nmap, NSLAB, TKD, LANES), BF16),
            pltpu.VMEM((nmap, TQ), F32),
            pltpu.VMEM((nmap, TQ), F32),
            pltpu.VMEM((nmap, TQ), F32),
            pltpu.VMEM((nmap, VD_ROWS, TQ), F32),
        ],
    )
    return pl.pallas_call(
        _make_diff_kernel(rix),
        grid_spec=grid_spec,
        out_shape=jax.ShapeDtypeStruct((B, DIFF_WIDTH, S), BF16),
        compiler_params=pltpu.CompilerParams(
            dimension_semantics=("arbitrary",), vmem_limit_bytes=VMEM_LIMIT),
        name="diff_attn",
    )(jnp.asarray(tab), slopes2, lq1, lk1, lq2, lk2, subln_g_col, jnp.asarray(bias),
      q3, klo3, khi3, vd)


SB_Q, SB_K, SB_MSEL, SB_KEEP, SB_BLK, SB_NEXT = range(6)
SB_DEAD = 160.0


def _make_sb_kernel(counts):
    pairs = range(N_SB_PAIRS)

    def kernel(tab_ref, g_ref, mask_ref, stair_ref, qt_ref, klo_ref, khi_ref, vs_ref, o_ref,
               z_scr, sp_scr, a_scr, acc_scr, c_scr):
        su = lax.broadcasted_iota(jnp.int32, (2 * TKS, 2 * TKS), 0)
        ju = lax.broadcasted_iota(jnp.int32, (2 * TKS, 2 * TKS), 1)
        utbd = jnp.where((ju >= su) & ((ju // TKS) == (su // TKS)), 1.0, 0.0).astype(BF16)

        z_scr[...] = jnp.full(z_scr.shape, NEG_INF, F32)
        sp_scr[...] = jnp.zeros(sp_scr.shape, BF16)
        a_scr[...] = jnp.zeros(a_scr.shape, BF16)
        acc_scr[...] = jnp.zeros(acc_scr.shape, F32)
        c_scr[...] = jnp.zeros(c_scr.shape, F32)

        def half_step(st, x_nxt, x_inf, dead_x, y_inf):
            ot = 1 - st
            nx, ny = counts[st], counts[ot]
            fx = lambda f, j: tab_ref[st * SB_FIELDS + f, j]
            fy = lambda f, j: tab_ref[ot * SB_FIELDS + f, j]
            jn = jnp.minimum(x_nxt, nx - 1)
            ji = jnp.minimum(x_inf, nx - 1)
            jy = jnp.minimum(y_inf, ny - 1)
            same = (x_inf < nx) & (x_nxt < nx) & (fx(SB_BLK, jn) == fx(SB_BLK, ji))
            x1 = jnp.minimum(jnp.where((dead_x == 1) & same, fx(SB_NEXT, ji), x_nxt), nx)
            j1 = jnp.minimum(x1, nx - 1)

            ks = pl.multiple_of(fx(SB_K, ji), TKS)
            for p in pairs:
                rows = slice(p * LANES, (p + 1) * LANES)
                vbd = jnp.concatenate([vs_ref[rows, pl.ds(ks, TKS)],
                                       vs_ref[SB_WIDTH + p * LANES:SB_WIDTH + (p + 1) * LANES,
                                              pl.ds(ks, TKS)]], axis=1)
                slot = st * N_SB_PAIRS + p
                acc_scr[slot] = acc_scr[slot] + jnp.dot(vbd, a_scr[p], preferred_element_type=F32)

            keep2 = fy(SB_KEEP, jy).astype(F32)
            cmin = None
            for p in pairs:
                ra = ot * N_SB_HEADS + 2 * p
                suf = jnp.dot(utbd, sp_scr[p], preferred_element_type=F32)
                ca = c_scr[ra:ra + 1, :] * keep2
                cb = c_scr[ra + 1:ra + 2, :] * keep2
                cpair = jnp.concatenate([jnp.broadcast_to(ca, (TKS, TQ)),
                                         jnp.broadcast_to(cb, (TKS, TQ))], axis=0)
                a_scr[p] = jnp.exp2((z_scr[p] - suf - cpair).astype(BF16))
                ca = ca + suf[0:1, :]
                cb = cb + suf[TKS:TKS + 1, :]
                c_scr[ra:ra + 1, :] = ca
                c_scr[ra + 1:ra + 2, :] = cb
                m = jnp.minimum(ca, cb)
                cmin = m if cmin is None else jnp.minimum(cmin, m)
            dead_y = ((jnp.min(cmin) >= SB_DEAD) & (y_inf < ny)).astype(jnp.int32)

            qs = pl.multiple_of(fx(SB_Q, j1), TQ)
            ks = pl.multiple_of(fx(SB_K, j1), TKS)
            qcode = mask_ref[fx(SB_MSEL, j1)]
            kstair = stair_ref[...]
            for p in pairs:
                cols = slice(p * LANES, (p + 1) * LANES)
                qt = jnp.concatenate([qt_ref[cols, pl.ds(qs, TQ)], qcode], axis=0)
                kbd = jnp.concatenate([klo_ref[0, pl.ds(ks, TKS), cols],
                                       khi_ref[0, pl.ds(ks, TKS), cols]], axis=0)
                kbd = jnp.concatenate([kbd, kstair], axis=1)
                z = jnp.dot(kbd, qt, preferred_element_type=F32)
                sp = jnp.maximum(jnp.log2(1.0 + jnp.exp2(jnp.minimum(z, 126.0))), z)
                z_scr[p] = z
                sp_scr[p] = sp.astype(BF16)

            last3 = (x_inf < nx) & ((x1 >= nx) | (fx(SB_BLK, j1) != fx(SB_BLK, ji)))

            @pl.when(last3)
            def _():
                qf = pl.multiple_of(fx(SB_Q, ji), TQ)
                for p in pairs:
                    slot = st * N_SB_PAIRS + p
                    o = acc_scr[slot]
                    o2 = o * o
                    msa = jnp.mean(o2[:HEAD_DIM], axis=0, keepdims=True)
                    msb = jnp.mean(o2[HEAD_DIM:], axis=0, keepdims=True)
                    ms = jnp.concatenate([jnp.broadcast_to(msa, (HEAD_DIM, TQ)),
                                          jnp.broadcast_to(msb, (HEAD_DIM, TQ))], axis=0)
                    o = o * lax.rsqrt(ms + EPS) * g_ref[...]
                    o_ref[0, p * LANES:(p + 1) * LANES, pl.ds(qf, TQ)] = o.astype(BF16)
                    acc_scr[slot] = jnp.zeros((LANES, TQ), F32)

            x_nxt2 = jnp.where(x1 < nx, jnp.minimum(x1 + 1, nx), nx)
            return x_nxt2, x1, dead_y

        def both(carry):
            a_nxt, a_inf, dead_a, b_nxt, b_inf, dead_b = carry
            a_nxt, a_inf, dead_b = half_step(0, a_nxt, a_inf, dead_a, b_inf)
            b_nxt, b_inf, dead_a = half_step(1, b_nxt, b_inf, dead_b, a_inf)
            return a_nxt, a_inf, dead_a, b_nxt, b_inf, dead_b

        def busy(carry):
            a_nxt, a_inf, _, b_nxt, b_inf, _ = carry
            return ((a_nxt < counts[0]) | (a_inf < counts[0])
                    | (b_nxt < counts[1]) | (b_inf < counts[1]))

        zero = jnp.int32(0)
        lax.while_loop(busy, both, (zero, jnp.int32(counts[0]), zero,
                                    zero, jnp.int32(counts[1]), zero))

    return kernel


def _stick_breaking(qt, klo3, khi3, vs, g_pair_col):
    B, S, _ = klo3.shape
    tab, counts = _sb_schedule(S)
    nd = TQ // TKS
    r = np.arange(TQ)[:, None]
    x = np.arange(TKS)[None, :]
    codes = []
    for d in reversed(range(nd)):
        rel = r - d * TKS
        codes.append(((np.maximum(rel, 0) == x) & (rel < TKS)).astype(np.float32))
    codes.append(np.zeros((TQ, TKS), np.float32))
    mask = jnp.asarray(np.stack(codes).transpose(0, 2, 1), BF16)
    c = (np.arange(2 * TKS) % TKS)[:, None]
    stair = jnp.asarray(np.where(c >= x, NEG_INF, 0.0), BF16)
    seq_blk = pl.BlockSpec((1, S, SB_WIDTH), lambda b, tab: (b, 0, 1))
    tile = (N_SB_PAIRS, 2 * TKS, TQ)
    grid_spec = pltpu.PrefetchScalarGridSpec(
        num_scalar_prefetch=1,
        grid=(B,),
        in_specs=[
            pl.BlockSpec((LANES, 1), lambda b, tab: (0, 0)),
            pl.BlockSpec((nd + 1, TKS, TQ), lambda b, tab: (0, 0, 0)),
            pl.BlockSpec((2 * TKS, TKS), lambda b, tab: (0, 0)),
            pl.BlockSpec((SB_WIDTH, S), lambda b, tab: (0, b)),
            seq_blk, seq_blk,
            pl.BlockSpec((2 * SB_WIDTH, S), lambda b, tab: (0, b)),
        ],
        out_specs=pl.BlockSpec((1, SB_WIDTH, S), lambda b, tab: (b, 0, 0)),
        scratch_shapes=[
            pltpu.VMEM(tile, F32),
            pltpu.VMEM(tile, BF16),
            pltpu.VMEM(tile, BF16),
            pltpu.VMEM((2 * N_SB_PAIRS, LANES, TQ), F32),
            pltpu.VMEM((2 * N_SB_HEADS, TQ), F32),
        ],
    )
    return pl.pallas_call(
        _make_sb_kernel(counts),
        grid_spec=grid_spec,
        out_shape=jax.ShapeDtypeStruct((B, SB_WIDTH, S), BF16),
        compiler_params=pltpu.CompilerParams(
            dimension_semantics=("arbitrary",), vmem_limit_bytes=VMEM_LIMIT),
        name="stick_break",
    )(jnp.asarray(tab), g_pair_col, mask, stair, qt, klo3, khi3, vs)


def _out_ffn_kernel(x_ref, od_ref, os_ref, wo_ref, g2_ref, wu_ref, wd_ref, gf_ref, o_ref):
    tn = (((0,), (0,)), ((), ()))
    h = (x_ref[...]
         + lax.dot_general(od_ref[0], wo_ref[:DIFF_WIDTH, :], tn, preferred_element_type=F32)
         + lax.dot_general(os_ref[0], wo_ref[DIFF_WIDTH:, :], tn, preferred_element_type=F32))
    ms = jnp.mean(h * h, axis=-1, keepdims=True)
    n2 = (h * lax.rsqrt(ms + EPS) * g2_ref[...]).astype(BF16)
    u = jnp.dot(n2, wu_ref[...], preferred_element_type=F32)
    u = jnp.square(jnp.maximum(u, 0.0)).astype(BF16)
    h = h + jnp.dot(u, wd_ref[...], preferred_element_type=F32)
    ms = jnp.mean(h * h, axis=-1, keepdims=True)
    o_ref[...] = h * lax.rsqrt(ms + EPS) * gf_ref[...]


def _out_ffn(x2, od, osb, wo, g2, wu, wd, gf):
    rows = x2.shape[0]
    per_seq = od.shape[2] // TM
    const = lambda i: (0, 0)
    return pl.pallas_call(
        _out_ffn_kernel,
        grid=(rows // TM,),
        in_specs=[
            pl.BlockSpec((TM, D_MODEL), lambda i: (i, 0)),
            pl.BlockSpec((1, DIFF_WIDTH, TM), lambda i: (i // per_seq, 0, i % per_seq)),
            pl.BlockSpec((1, SB_WIDTH, TM), lambda i: (i // per_seq, 0, i % per_seq)),
            pl.BlockSpec((D_MODEL, D_MODEL), const),
            pl.BlockSpec((1, D_MODEL), const),
            pl.BlockSpec((D_MODEL, D_FF), const),
            pl.BlockSpec((D_FF, D_MODEL), const),
            pl.BlockSpec((1, D_MODEL), const),
        ],
        out_specs=pl.BlockSpec((TM, D_MODEL), lambda i: (i, 0)),
        out_shape=jax.ShapeDtypeStruct((rows, D_MODEL), F32),
        compiler_params=pltpu.CompilerParams(
            dimension_semantics=("arbitrary",), vmem_limit_bytes=VMEM_LIMIT),
        name="out_ffn",
    )(x2, od, osb, wo, g2, wu, wd, gf)


def kernel(x, norm1_g, w_in, lambda_q1, lambda_k1, lambda_q2, lambda_k2, diff_subln_g,
           sb_norm_g, w_out, norm2_g, w_up, w_down, final_norm_g):
    B, S, D = x.shape
    x2 = x.reshape(B * S, D)
    slopes2 = (np.exp2(-8.0 * np.arange(1, N_DIFF_HEADS + 1) / N_DIFF_HEADS) * LOG2E).astype(np.float32)

    q, qt, klo, khi, vd, vs, wu, wd, wo = _norm_proj(x2, norm1_g[0][None, :], w_in[0], w_up[0],
                                                 w_down[0], w_out[0])
    q3 = q.reshape(B, S, DIFF_WIDTH)
    klo3, khi3 = (a.reshape(B, S, Q_WIDTH) for a in (klo, khi))

    o_diff = _diff_attention(q3, klo3, khi3, vd, slopes2, lambda_q1, lambda_k1, lambda_q2,
                             lambda_k2, diff_subln_g[0][:, None])
    g_pair = jnp.concatenate([sb_norm_g[0], sb_norm_g[0]])[:, None]
    o_sb = _stick_breaking(qt, klo3, khi3, vs, g_pair)

    out = _out_ffn(x2, o_diff, o_sb, wo, norm2_g[0][None, :], wu, wd, final_norm_g[None, :])
    return out.reshape(B, S, D)
```

```python
import math

import numpy as np
import jax
import jax.numpy as jnp
from jax import lax
from jax.experimental import pallas as pl
from jax.experimental.pallas import tpu as pltpu

D_MODEL = 1024
HEAD_DIM = 64
CHUNK = 64
N_DIFF_HEADS = 4
N_SB_HEADS = 8
N_SB_PAIRS = N_SB_HEADS // 2
DIFF_V_DIM = 2 * HEAD_DIM
DIFF_WIDTH = N_DIFF_HEADS * DIFF_V_DIM
SB_WIDTH = N_SB_HEADS * HEAD_DIM
Q_WIDTH = DIFF_WIDTH + SB_WIDTH
D_FF = 4 * D_MODEL
EPS = 1e-6
SB_FIELDS = 6
NEG_INF = -1e30
LAMBDA_INIT = 0.8 - 0.6 * math.exp(-0.3 * 0)
LOG2E = 1.4426950408889634

LANES = 128
BF16_ROWS = 16
TQ = 256
TKD = 256
TKS = 128
NSLAB = TQ // LANES
TM = 512
TM_PROJ = 1024
VD_ROWS = DIFF_V_DIM + BF16_ROWS
VMEM_LIMIT = 56 * 1024 * 1024

F32 = jnp.float32
BF16 = jnp.bfloat16


def _nt_dot(a, b):
    return lax.dot_general(a, b, (((1,), (1,)), ((), ())), preferred_element_type=F32)


def _norm_proj_kernel(x_ref, g_ref, win_ref, wu_ref, wd_ref, wo_ref,
                      oq_ref, oqt_ref, oklo_ref, okhi_ref, ovd_ref, ovs_ref, owu_ref, owd_ref,
                      owo_ref, wq_ref, wqt_ref, wk_ref, wvt_ref):
    @pl.when(pl.program_id(0) == 0)
    def _():
        d1, d2, d3 = DIFF_WIDTH, 2 * DIFF_WIDTH, 3 * DIFF_WIDTH
        s1, s2 = d3 + SB_WIDTH, d3 + 2 * SB_WIDTH
        wq_ref[...] = win_ref[:, :d1].astype(BF16)
        wqt_ref[...] = win_ref[:, d3:s1].T.astype(BF16)
        wk_ref[:, :DIFF_WIDTH] = win_ref[:, d1:d2].astype(BF16)
        wk_ref[:, DIFF_WIDTH:] = win_ref[:, s1:s2].astype(BF16)
        wvt_ref[:DIFF_WIDTH, :] = win_ref[:, d2:d3].T.astype(BF16)
        wvt_ref[DIFF_WIDTH:, :] = win_ref[:, s2:].T.astype(BF16)

    owu_ref[...] = wu_ref[...].astype(BF16)
    owd_ref[...] = wd_ref[...].astype(BF16)
    owo_ref[...] = wo_ref[...].astype(BF16)

    x = x_ref[...]
    ms = jnp.mean(x * x, axis=-1, keepdims=True)
    n = (x * lax.rsqrt(ms + EPS) * g_ref[...]).astype(BF16)
    chunk = 512
    lane = lax.broadcasted_iota(jnp.int32, (1, chunk), 1) % LANES
    lo = jnp.where(lane < HEAD_DIM, 1.0, 0.0)
    hi = 1.0 - lo
    qscale = LOG2E / math.sqrt(HEAD_DIM)
    oq_ref[...] = (jnp.dot(n, wq_ref[...], preferred_element_type=F32) * qscale).astype(BF16)
    oqt_ref[...] = (_nt_dot(wqt_ref[...], n) * qscale).astype(BF16)
    for c in range(Q_WIDTH // chunk):
        sl = slice(c * chunk, (c + 1) * chunk)
        kk = jnp.dot(n, wk_ref[:, sl], preferred_element_type=F32)
        oklo_ref[:, sl] = (kk * lo).astype(BF16)
        okhi_ref[:, sl] = (kk * hi).astype(BF16)
    vd = _nt_dot(wvt_ref[:DIFF_WIDTH, :], n)
    prow = lax.broadcasted_iota(jnp.int32, (BF16_ROWS, TM_PROJ), 0)
    ones_rows = jnp.where(prow == 0, 1.0, 0.0).astype(BF16)
    for h in range(N_DIFF_HEADS):
        ovd_ref[h * VD_ROWS:h * VD_ROWS + DIFF_V_DIM, :] = (
            vd[h * DIFF_V_DIM:(h + 1) * DIFF_V_DIM].astype(BF16))
        ovd_ref[h * VD_ROWS + DIFF_V_DIM:(h + 1) * VD_ROWS, :] = ones_rows
    vs = _nt_dot(wvt_ref[DIFF_WIDTH:, :], n)
    row = lax.broadcasted_iota(jnp.int32, (SB_WIDTH, 1), 0) % LANES
    ra = jnp.where(row < HEAD_DIM, 1.0, 0.0)
    ovs_ref[:SB_WIDTH, :] = (vs * ra).astype(BF16)
    ovs_ref[SB_WIDTH:, :] = (vs * (1.0 - ra)).astype(BF16)


def _norm_proj(x2, g, w_in, w_up, w_down, w_out):
    rows = x2.shape[0]
    steps = rows // TM_PROJ
    const = lambda i: (0, 0)
    row_blk = lambda w: pl.BlockSpec((TM_PROJ, w), lambda i: (i, 0))
    col_blk = lambda r: pl.BlockSpec((r, TM_PROJ), lambda i: (0, i))
    wu_blk = pl.BlockSpec((D_MODEL, D_FF // steps), lambda i: (0, i))
    wd_blk = pl.BlockSpec((D_FF // steps, D_MODEL), lambda i: (i, 0))
    wo_blk = pl.BlockSpec((D_MODEL // steps, D_MODEL), lambda i: (i, 0))
    return pl.pallas_call(
        _norm_proj_kernel,
        grid=(steps,),
        in_specs=[
            row_blk(D_MODEL),
            pl.BlockSpec((1, D_MODEL), const),
            pl.BlockSpec(w_in.shape, const),
            wu_blk, wd_blk, wo_blk,
        ],
        out_specs=[row_blk(DIFF_WIDTH), col_blk(SB_WIDTH), row_blk(Q_WIDTH), row_blk(Q_WIDTH),
                   col_blk(N_DIFF_HEADS * VD_ROWS), col_blk(2 * SB_WIDTH),
                   wu_blk, wd_blk, wo_blk],
        out_shape=[jax.ShapeDtypeStruct((rows, DIFF_WIDTH), BF16),
                   jax.ShapeDtypeStruct((SB_WIDTH, rows), BF16),
                   jax.ShapeDtypeStruct((rows, Q_WIDTH), BF16),
                   jax.ShapeDtypeStruct((rows, Q_WIDTH), BF16),
                   jax.ShapeDtypeStruct((N_DIFF_HEADS * VD_ROWS, rows), BF16),
                   jax.ShapeDtypeStruct((2 * SB_WIDTH, rows), BF16),
                   jax.ShapeDtypeStruct(w_up.shape, BF16),
                   jax.ShapeDtypeStruct(w_down.shape, BF16),
                   jax.ShapeDtypeStruct(w_out.shape, BF16)],
        scratch_shapes=[pltpu.VMEM((D_MODEL, DIFF_WIDTH), BF16),
                        pltpu.VMEM((SB_WIDTH, D_MODEL), BF16),
                        pltpu.VMEM((D_MODEL, Q_WIDTH), BF16),
                        pltpu.VMEM((Q_WIDTH, D_MODEL), BF16)],
        compiler_params=pltpu.CompilerParams(
            dimension_semantics=("arbitrary",), vmem_limit_bytes=VMEM_LIMIT),
        name="norm_proj",
    )(x2, g, w_in, w_up, w_down, w_out)


def _stage_table(items, lags, fields, defaults):
    n = len(items)
    nsteps = n + max(lags)
    nsteps += nsteps % 2
    rows = [(s, f) for s in range(len(lags)) for f in fields[s]]
    tab = np.zeros((len(rows), nsteps), np.int32)
    for ri, (s, f) in enumerate(rows):
        for t in range(nsteps):
            idx = t - lags[s]
            if 0 <= idx < n:
                tab[ri, t] = items[idx][f]
            elif f in defaults:
                tab[ri, t] = defaults[f]
            else:
                tab[ri, t] = items[min(max(idx, 0), n - 1)][f]
    return tab, {(s, f): ri for ri, (s, f) in enumerate(rows)}


def _diff_schedule(seq):
    items = []
    for i in range(seq // TQ):
        t0 = i * TQ
        items.append(dict(q=t0, k=t0, kind=0, keep=0, dt=0, last=0))
        for j in range(i):
            items.append(dict(q=t0, k=j * TKD, kind=1, keep=1, dt=t0 - j * TKD, last=0))
        items[-1]["last"] = 1
    return _stage_table(items, lags=(0, 1, 2),
                        fields=(("q", "k"), ("kind", "keep", "dt"), ("k", "q", "last")),
                        defaults=dict(keep=1, last=0))


def _sb_schedule(seq):
    nd = TQ // TKS
    streams = [[], []]
    for i in range(seq // TQ):
        items = streams[i % 2]
        t0 = i * TQ
        first = len(items)
        for d in reversed(range(nd)):
            items.append([t0, t0 + d * TKS, nd - 1 - d, 1, i, 0])
        for j in reversed(range(nd * i)):
            items.append([t0, j * TKS, nd, 1, i, 0])
        items[first][3] = 0
        for it in items[first:]:
            it[5] = len(items)
    counts = tuple(len(items) for items in streams)
    tab = np.zeros((2 * SB_FIELDS, max(counts)), np.int32)
    for st, items in enumerate(streams):
        arr = np.asarray(items, np.int32).T
        tab[st * SB_FIELDS:(st + 1) * SB_FIELDS, :counts[st]] = arr
        tab[st * SB_FIELDS:(st + 1) * SB_FIELDS, counts[st]:] = arr[:, -1:]
    return tab, counts


def _make_diff_kernel(rix):
    heads = range(N_DIFF_HEADS)

    def kernel(tab_ref, slopes_ref, lq1_ref, lk1_ref, lq2_ref, lk2_ref, g_ref, bias_ref,
               q_ref, klo_ref, khi_ref, vd_ref, o_ref,
               s0_scr, s1_scr, p0_scr, p1_scr, al0_scr, al1_scr, m_scr, acc_scr):
        s_scr, p_scr, alpha_scr = (s0_scr, s1_scr), (p0_scr, p1_scr), (al0_scr, al1_scr)
        lam = (jnp.exp(jnp.sum(lq1_ref[...] * lk1_ref[...], axis=1, keepdims=True))
               - jnp.exp(jnp.sum(lq2_ref[...] * lk2_ref[...], axis=1, keepdims=True))
               + LAMBDA_INIT)

        @pl.when(pl.program_id(0) == 0)
        def _():
            for ref in s_scr + p_scr + alpha_scr:
                ref[...] = jnp.zeros(ref.shape, ref.dtype)
            m_scr[...] = jnp.full(m_scr.shape, NEG_INF, F32)
            acc_scr[...] = jnp.zeros(acc_scr.shape, F32)

        def step(t, cur):
            prev = 1 - cur
            qs = pl.multiple_of(tab_ref[rix[0, "q"], t], TQ)
            ks = pl.multiple_of(tab_ref[rix[0, "k"], t], TKD)
            for h in heads:
                cols = slice(h * LANES, (h + 1) * LANES)
                q = q_ref[0, pl.ds(qs, TQ), cols]
                kbd = jnp.concatenate([klo_ref[0, pl.ds(ks, TKD), cols],
                                       khi_ref[0, pl.ds(ks, TKD), cols]], axis=0)
                s = _nt_dot(kbd, q)
                for half in range(NSLAB):
                    s_scr[cur][h, half] = s[:, half * LANES:(half + 1) * LANES]

            ks = pl.multiple_of(tab_ref[rix[2, "k"], t], TKD)
            for h in heads:
                vt = vd_ref[h * VD_ROWS:(h + 1) * VD_ROWS, pl.ds(ks, TKD)]
                for m in range(2):
                    row = 2 * h + m
                    p = jnp.concatenate([p_scr[prev][row, half] for half in range(NSLAB)], axis=1)
                    pv = jnp.dot(vt, p, preferred_element_type=F32)
                    acc_scr[row] = acc_scr[row] * alpha_scr[prev][row:row + 1, :] + pv

            kind = tab_ref[rix[1, "kind"], t]
            keep = tab_ref[rix[1, "keep"], t].astype(F32)
            dt = tab_ref[rix[1, "dt"], t].astype(F32)
            for h in heads:
                shift = -slopes_ref[h] * dt
                for m in range(2):
                    row = 2 * h + m
                    for half in range(NSLAB):
                        ls = slice(half * LANES, (half + 1) * LANES)
                        u = (s_scr[prev][h, half, m * TKD:(m + 1) * TKD, :]
                             + bias_ref[h, kind, half])
                        m_old = m_scr[row:row + 1, ls] * keep + NEG_INF * (1.0 - keep)
                        m_new = jnp.maximum(m_old, jnp.max(u, axis=0, keepdims=True) + shift)
                        p_scr[cur][row, half] = jnp.exp2(u - (m_new - shift)).astype(BF16)
                        alpha_scr[cur][row:row + 1, ls] = jnp.exp2(m_old - m_new)
                        m_scr[row:row + 1, ls] = m_new

            @pl.when(tab_ref[rix[2, "last"], t] == 1)
            def _():
                qf = pl.multiple_of(tab_ref[rix[2, "q"], t], TQ)
                gain = g_ref[...] * (1.0 - LAMBDA_INIT)
                for h in heads:
                    a1 = acc_scr[2 * h]
                    a2 = acc_scr[2 * h + 1]
                    r1 = 1.0 / a1[DIFF_V_DIM:DIFF_V_DIM + 1]
                    r2 = lam / a2[DIFF_V_DIM:DIFF_V_DIM + 1]
                    o = a1[:DIFF_V_DIM] * r1 - a2[:DIFF_V_DIM] * r2
                    ms = jnp.mean(o * o, axis=0, keepdims=True)
                    o = o * lax.rsqrt(ms + EPS) * gain
                    o_ref[0, h * LANES:(h + 1) * LANES, pl.ds(qf, TQ)] = o.astype(BF16)

        def two_steps(i, carry):
            step(2 * i, 0)
            step(2 * i + 1, 1)
            return carry

        lax.fori_loop(0, tab_ref.shape[1] // 2, two_steps, 0)

    return kernel


def _diff_attention(q3, klo3, khi3, vd, slopes2, lq1, lk1, lq2, lk2, subln_g_col):
    B, S, _ = q3.shape
    tab, rix = _diff_schedule(S)
    c = np.arange(TKD)[:, None]
    r = np.arange(TQ)[None, :]
    slopes = np.asarray(slopes2, np.float64)[:, None, None]
    diag = np.where((c // CHUNK) <= (r // CHUNK), -slopes * np.abs(r - c), NEG_INF)
    full = -slopes * (r - c)
    bias = np.stack([diag, full], axis=1).astype(np.float32)
    bias = bias.reshape(N_DIFF_HEADS, 2, TKD, NSLAB, LANES).transpose(0, 1, 3, 2, 4)
    vec = pl.BlockSpec((1, HEAD_DIM), lambda b, tab: (0, 0))
    seq_blk = pl.BlockSpec((1, S, DIFF_WIDTH), lambda b, tab: (b, 0, 0))
    nmap = 2 * N_DIFF_HEADS
    grid_spec = pltpu.PrefetchScalarGridSpec(
        num_scalar_prefetch=1,
        grid=(B,),
        in_specs=[
            pl.BlockSpec(memory_space=pltpu.SMEM),
            vec, vec, vec, vec,
            pl.BlockSpec((DIFF_V_DIM, 1), lambda b, tab: (0, 0)),
            pl.BlockSpec((N_DIFF_HEADS, 2, NSLAB, TKD, LANES), lambda b, tab: (0, 0, 0, 0, 0)),
            seq_blk, seq_blk, seq_blk,
            pl.BlockSpec((N_DIFF_HEADS * VD_ROWS, S), lambda b, tab: (0, b)),
        ],
        out_specs=pl.BlockSpec((1, DIFF_WIDTH, S), lambda b, tab: (b, 0, 0)),
        scratch_shapes=[
            pltpu.VMEM((N_DIFF_HEADS, NSLAB, 2 * TKD, LANES), F32),
            pltpu.VMEM((N_DIFF_HEADS, NSLAB, 2 * TKD, LANES), F32),
            pltpu.VMEM((nmap, NSLAB, TKD, LANES), BF16),
            pltpu.VMEM((nmap, NSLAB, TKD, LANES), BF16),
            pltpu.VMEM((nmap, TQ), F32),
            pltpu.VMEM((nmap, TQ), F32),
            pltpu.VMEM((2 * nmap, TQ), F32),
            pltpu.VMEM((nmap, VD_ROWS, TQ), F32),
        ],
    )
    return pl.pallas_call(
        _make_diff_kernel(rix),
        grid_spec=grid_spec,
        out_shape=jax.ShapeDtypeStruct((B, DIFF_WIDTH, S), BF16),
        compiler_params=pltpu.CompilerParams(
            dimension_semantics=("arbitrary",), vmem_limit_bytes=VMEM_LIMIT),
        name="diff_attn",
    )(jnp.asarray(tab), slopes2, lq1, lk1, lq2, lk2, subln_g_col, jnp.asarray(bias),
      q3, klo3, khi3, vd)


SB_Q, SB_K, SB_MSEL, SB_KEEP, SB_BLK, SB_NEXT = range(6)
SB_DEAD = 160.0


def _make_sb_kernel(counts):
    pairs = range(N_SB_PAIRS)

    def kernel(tab_ref, g_ref, mask_ref, stair_ref, qt_ref, klo_ref, khi_ref, vs_ref, o_ref,
               z_scr, sp_scr, a_scr, acc_scr, c_scr):
        su = lax.broadcasted_iota(jnp.int32, (2 * TKS, 2 * TKS), 0)
        ju = lax.broadcasted_iota(jnp.int32, (2 * TKS, 2 * TKS), 1)
        utbd = jnp.where((ju >= su) & ((ju // TKS) == (su // TKS)), 1.0, 0.0).astype(BF16)

        z_scr[...] = jnp.full(z_scr.shape, NEG_INF, F32)
        sp_scr[...] = jnp.zeros(sp_scr.shape, BF16)
        a_scr[...] = jnp.zeros(a_scr.shape, BF16)
        acc_scr[...] = jnp.zeros(acc_scr.shape, F32)
        c_scr[...] = jnp.zeros(c_scr.shape, F32)

        def half_step(st, x_nxt, x_inf, dead_x, y_inf):
            ot = 1 - st
            nx, ny = counts[st], counts[ot]
            fx = lambda f, j: tab_ref[st * SB_FIELDS + f, j]
            fy = lambda f, j: tab_ref[ot * SB_FIELDS + f, j]
            jn = jnp.minimum(x_nxt, nx - 1)
            ji = jnp.minimum(x_inf, nx - 1)
            jy = jnp.minimum(y_inf, ny - 1)
            same = (x_inf < nx) & (x_nxt < nx) & (fx(SB_BLK, jn) == fx(SB_BLK, ji))
            x1 = jnp.minimum(jnp.where((dead_x == 1) & same, fx(SB_NEXT, ji), x_nxt), nx)
            j1 = jnp.minimum(x1, nx - 1)

            ks = pl.multiple_of(fx(SB_K, ji), TKS)
            for p in pairs:
                rows = slice(p * LANES, (p + 1) * LANES)
                vbd = jnp.concatenate([vs_ref[rows, pl.ds(ks, TKS)],
                                       vs_ref[SB_WIDTH + p * LANES:SB_WIDTH + (p + 1) * LANES,
                                              pl.ds(ks, TKS)]], axis=1)
                slot = st * N_SB_PAIRS + p
                acc_scr[slot] = acc_scr[slot] + jnp.dot(vbd, a_scr[p], preferred_element_type=F32)

            keep2 = fy(SB_KEEP, jy).astype(F32)
            cmin = None
            for p in pairs:
                ra = ot * N_SB_HEADS + 2 * p
                suf = jnp.dot(utbd, sp_scr[p], preferred_element_type=F32)
                ca = c_scr[ra:ra + 1, :] * keep2
                cb = c_scr[ra + 1:ra + 2, :] * keep2
                cpair = jnp.concatenate([jnp.broadcast_to(ca, (TKS, TQ)),
                                         jnp.broadcast_to(cb, (TKS, TQ))], axis=0)
                a_scr[p] = jnp.exp2((z_scr[p] - suf - cpair).astype(BF16))
                ca = ca + suf[0:1, :]
                cb = cb + suf[TKS:TKS + 1, :]
                c_scr[ra:ra + 1, :] = ca
                c_scr[ra + 1:ra + 2, :] = cb
                m = jnp.minimum(ca, cb)
                cmin = m if cmin is None else jnp.minimum(cmin, m)
            dead_y = ((jnp.min(cmin) >= SB_DEAD) & (y_inf < ny)).astype(jnp.int32)

            qs = pl.multiple_of(fx(SB_Q, j1), TQ)
            ks = pl.multiple_of(fx(SB_K, j1), TKS)
            qcode = mask_ref[fx(SB_MSEL, j1)]
            kstair = stair_ref[...]
            for p in pairs:
                cols = slice(p * LANES, (p + 1) * LANES)
                qt = jnp.concatenate([qt_ref[cols, pl.ds(qs, TQ)], qcode], axis=0)
                kbd = jnp.concatenate([klo_ref[0, pl.ds(ks, TKS), cols],
                                       khi_ref[0, pl.ds(ks, TKS), cols]], axis=0)
                kbd = jnp.concatenate([kbd, kstair], axis=1)
                z = jnp.dot(kbd, qt, preferred_element_type=F32)
                sp = jnp.maximum(jnp.log2(1.0 + jnp.exp2(jnp.minimum(z, 126.0))), z)
                z_scr[p] = z
                sp_scr[p] = sp.astype(BF16)

            last3 = (x_inf < nx) & ((x1 >= nx) | (fx(SB_BLK, j1) != fx(SB_BLK, ji)))

            @pl.when(last3)
            def _():
                qf = pl.multiple_of(fx(SB_Q, ji), TQ)
                for p in pairs:
                    slot = st * N_SB_PAIRS + p
                    o = acc_scr[slot]
                    o2 = o * o
                    msa = jnp.mean(o2[:HEAD_DIM], axis=0, keepdims=True)
                    msb = jnp.mean(o2[HEAD_DIM:], axis=0, keepdims=True)
                    ms = jnp.concatenate([jnp.broadcast_to(msa, (HEAD_DIM, TQ)),
                                          jnp.broadcast_to(msb, (HEAD_DIM, TQ))], axis=0)
                    o = o * lax.rsqrt(ms + EPS) * g_ref[...]
                    o_ref[0, p * LANES:(p + 1) * LANES, pl.ds(qf, TQ)] = o.astype(BF16)
                    acc_scr[slot] = jnp.zeros((LANES, TQ), F32)

            x_nxt2 = jnp.where(x1 < nx, jnp.minimum(x1 + 1, nx), nx)
            return x_nxt2, x1, dead_y

        def both(carry):
            a_nxt, a_inf, dead_a, b_nxt, b_inf, dead_b = carry
            a_nxt, a_inf, dead_b = half_step(0, a_nxt, a_inf, dead_a, b_inf)
            b_nxt, b_inf, dead_a = half_step(1, b_nxt, b_inf, dead_b, a_inf)
            return a_nxt, a_inf, dead_a, b_nxt, b_inf, dead_b

        def busy(carry):
            a_nxt, a_inf, _, b_nxt, b_inf, _ = carry
            return ((a_nxt < counts[0]) | (a_inf < counts[0])
                    | (b_nxt < counts[1]) | (b_inf < counts[1]))

        zero = jnp.int32(0)
        lax.while_loop(busy, both, (zero, jnp.int32(counts[0]), zero,
                                    zero, jnp.int32(counts[1]), zero))

    return kernel


def _stick_breaking(qt, klo3, khi3, vs, g_pair_col):
    B, S, _ = klo3.shape
    tab, counts = _sb_schedule(S)
    nd = TQ // TKS
    r = np.arange(TQ)[:, None]
    x = np.arange(TKS)[None, :]
    codes = []
    for d in reversed(range(nd)):
        rel = r - d * TKS
        codes.append(((np.maximum(rel, 0) == x) & (rel < TKS)).astype(np.float32))
    codes.append(np.zeros((TQ, TKS), np.float32))
    mask = jnp.asarray(np.stack(codes).transpose(0, 2, 1), BF16)
    c = (np.arange(2 * TKS) % TKS)[:, None]
    stair = jnp.asarray(np.where(c >= x, NEG_INF, 0.0), BF16)
    seq_blk = pl.BlockSpec((1, S, SB_WIDTH), lambda b, tab: (b, 0, 1))
    tile = (N_SB_PAIRS, 2 * TKS, TQ)
    grid_spec = pltpu.PrefetchScalarGridSpec(
        num_scalar_prefetch=1,
        grid=(B,),
        in_specs=[
            pl.BlockSpec((LANES, 1), lambda b, tab: (0, 0)),
            pl.BlockSpec((nd + 1, TKS, TQ), lambda b, tab: (0, 0, 0)),
            pl.BlockSpec((2 * TKS, TKS), lambda b, tab: (0, 0)),
            pl.BlockSpec((SB_WIDTH, S), lambda b, tab: (0, b)),
            seq_blk, seq_blk,
            pl.BlockSpec((2 * SB_WIDTH, S), lambda b, tab: (0, b)),
        ],
        out_specs=pl.BlockSpec((1, SB_WIDTH, S), lambda b, tab: (b, 0, 0)),
        scratch_shapes=[
            pltpu.VMEM(tile, F32),
            pltpu.VMEM(tile, BF16),
            pltpu.VMEM(tile, BF16),
            pltpu.VMEM((2 * N_SB_PAIRS, LANES, TQ), F32),
            pltpu.VMEM((2 * N_SB_HEADS, TQ), F32),
        ],
    )
    return pl.pallas_call(
        _make_sb_kernel(counts),
        grid_spec=grid_spec,
        out_shape=jax.ShapeDtypeStruct((B, SB_WIDTH, S), BF16),
        compiler_params=pltpu.CompilerParams(
            dimension_semantics=("arbitrary",), vmem_limit_bytes=VMEM_LIMIT),
        name="stick_break",
    )(jnp.asarray(tab), g_pair_col, mask, stair, qt, klo3, khi3, vs)


def _out_ffn_kernel(x_ref, od_ref, os_ref, wo_ref, g2_ref, wu_ref, wd_ref, gf_ref, o_ref):
    tn = (((0,), (0,)), ((), ()))
    h = (x_ref[...]
         + lax.dot_general(od_ref[0], wo_ref[:DIFF_WIDTH, :], tn, preferred_element_type=F32)
         + lax.dot_general(os_ref[0], wo_ref[DIFF_WIDTH:, :], tn, preferred_element_type=F32))
    ms = jnp.mean(h * h, axis=-1, keepdims=True)
    n2 = (h * lax.rsqrt(ms + EPS) * g2_ref[...]).astype(BF16)
    u = jnp.dot(n2, wu_ref[...], preferred_element_type=F32)
    u = jnp.square(jnp.maximum(u, 0.0)).astype(BF16)
    h = h + jnp.dot(u, wd_ref[...], preferred_element_type=F32)
    ms = jnp.mean(h * h, axis=-1, keepdims=True)
    o_ref[...] = h * lax.rsqrt(ms + EPS) * gf_ref[...]


def _out_ffn(x2, od, osb, wo, g2, wu, wd, gf):
    rows = x2.shape[0]
    per_seq = od.shape[2] // TM
    const = lambda i: (0, 0)
    return pl.pallas_call(
        _out_ffn_kernel,
        grid=(rows // TM,),
        in_specs=[
            pl.BlockSpec((TM, D_MODEL), lambda i: (i, 0)),
            pl.BlockSpec((1, DIFF_WIDTH, TM), lambda i: (i // per_seq, 0, i % per_seq)),
            pl.BlockSpec((1, SB_WIDTH, TM), lambda i: (i // per_seq, 0, i % per_seq)),
            pl.BlockSpec((D_MODEL, D_MODEL), const),
            pl.BlockSpec((1, D_MODEL), const),
            pl.BlockSpec((D_MODEL, D_FF), const),
            pl.BlockSpec((D_FF, D_MODEL), const),
            pl.BlockSpec((1, D_MODEL), const),
        ],
        out_specs=pl.BlockSpec((TM, D_MODEL), lambda i: (i, 0)),
        out_shape=jax.ShapeDtypeStruct((rows, D_MODEL), F32),
        compiler_params=pltpu.CompilerParams(
            dimension_semantics=("arbitrary",), vmem_limit_bytes=VMEM_LIMIT),
        name="out_ffn",
    )(x2, od, osb, wo, g2, wu, wd, gf)


def kernel(x, norm1_g, w_in, lambda_q1, lambda_k1, lambda_q2, lambda_k2, diff_subln_g,
           sb_norm_g, w_out, norm2_g, w_up, w_down, final_norm_g):
    B, S, D = x.shape
    x2 = x.reshape(B * S, D)
    slopes2 = (np.exp2(-8.0 * np.arange(1, N_DIFF_HEADS + 1) / N_DIFF_HEADS) * LOG2E).astype(np.float32)

    q, qt, klo, khi, vd, vs, wu, wd, wo = _norm_proj(x2, norm1_g[0][None, :], w_in[0], w_up[0],
                                                 w_down[0], w_out[0])
    q3 = q.reshape(B, S, DIFF_WIDTH)
    klo3, khi3 = (a.reshape(B, S, Q_WIDTH) for a in (klo, khi))

    o_diff = _diff_attention(q3, klo3, khi3, vd, slopes2, lambda_q1, lambda_k1, lambda_q2,
                             lambda_k2, diff_subln_g[0][:, None])
    g_pair = jnp.concatenate([sb_norm_g[0], sb_norm_g[0]])[:, None]
    o_sb = _stick_breaking(qt, klo3, khi3, vs, g_pair)

    out = _out_ffn(x2, o_diff, o_sb, wo, norm2_g[0][None, :], wu, wd, final_norm_g[None, :])
    return out.reshape(B, S, D)
```
